```python
import jax, jax.numpy as jnp
from jax import lax
import numpy as np

D_MODEL = 4096
BATCH = 2
SEQ = 8192
DEPTH = 2

GRID_W = 64
CTX_LEN = 256
A_HEADS = 16
A_HEAD_DIM = 128
D_A = A_HEADS * A_HEAD_DIM
D_B = D_MODEL // 2
CONV_K = 31
CONV_PAD = (CONV_K - 1) // 2
FFN_MULT = 256
D_FF = -(-8 * D_MODEL // (3 * FFN_MULT)) * FFN_MULT
CHUNK = 64
EPS = 1e-6
POS_BASE = 10000.0
MOD_DIM = 6 * D_MODEL
P_IN = 5 * D_A + 2 * D_B + 2 * D_MODEL

kernel_name = 'hgrn2_conformer_parallel_dit_block'


def _rmsnorm(x, g):
    xf = x.astype(jnp.float32)
    y = xf * lax.rsqrt(jnp.mean(xf * xf, axis=-1, keepdims=True) + EPS)
    return (y * g).astype(x.dtype)


def _modulate(x, g, shift, scale):
    return _rmsnorm(x, g) * (1 + scale) + shift


def _grid_pos_embed(seq_len, dim):
    rows = seq_len // GRID_W
    r, cl = jnp.meshgrid(jnp.arange(rows), jnp.arange(GRID_W), indexing='ij')
    quarter = dim // 4
    omega = 1.0 / (POS_BASE ** (jnp.arange(quarter, dtype=jnp.float32) / quarter))
    ang_r = r.reshape(-1, 1).astype(jnp.float32) * omega
    ang_c = cl.reshape(-1, 1).astype(jnp.float32) * omega
    return jnp.concatenate([jnp.sin(ang_r), jnp.cos(ang_r), jnp.sin(ang_c), jnp.cos(ang_c)], axis=-1)


def _split_proj(z):
    idx = np.cumsum([D_A, D_A, D_A, D_A, D_A, 2 * D_B, D_MODEL])
    return jnp.split(z, idx, axis=-1)


def _heads(t):
    b, l, _ = t.shape
    return t.reshape(b, l, A_HEADS, A_HEAD_DIM).transpose(0, 2, 1, 3).astype(jnp.float32)


def _forget(f_logit, lb):
    log_f = jnp.logaddexp(jnp.log(lb), jnp.log1p(-lb) + jax.nn.log_sigmoid(f_logit))
    k = (1.0 - lb) * jax.nn.sigmoid(-f_logit)
    return log_f, k


def _chunk_scan(q, k, v, log_f, s0):
    b, h, seq_len, dk = q.shape
    dv = v.shape[-1]
    n = seq_len // CHUNK

    def to_chunks(t):
        return jnp.moveaxis(t.reshape(b, h, n, CHUNK, t.shape[-1]), 2, 0)

    causal = jnp.tril(jnp.ones((CHUNK, CHUNK), dtype=bool))[:, :, None]

    def step(s, inp):
        qc, kc, vc, gc = inp
        bcum = jnp.cumsum(gc, axis=2)
        diff = bcum[:, :, :, None, :] - bcum[:, :, None, :, :]
        decay = jnp.exp(jnp.where(causal, diff, -jnp.inf))
        scores = jnp.einsum('bhtk,bhsk,bhtsk->bhts', qc, kc, decay)
        o = (jnp.einsum('bhts,bhsv->bhtv', scores, vc)
             + jnp.einsum('bhtk,bhkv->bhtv', qc * jnp.exp(bcum), s))
        b_last = bcum[:, :, -1:, :]
        s_new = (jnp.exp(b_last[:, :, 0, :])[..., None] * s
                 + jnp.einsum('bhsk,bhsv->bhkv', kc * jnp.exp(b_last - bcum), vc))
        return s_new, o

    s_fin, o = lax.scan(step, s0, (to_chunks(q), to_chunks(k), to_chunks(v), to_chunks(log_f)))
    o = jnp.moveaxis(o, 0, 2).reshape(b, h, seq_len, dv)
    return o, s_fin


def _hgrn2_bidir(q, v, f_fw, f_bw, lb, s_fw, s_bw):
    q = _heads(jax.nn.silu(q))
    v = _heads(v)
    lf_fw, k_fw = _forget(_heads(f_fw), lb[0])
    lf_bw, k_bw = _forget(_heads(f_bw), lb[1])
    o_fw, s_fw = _chunk_scan(q, k_fw, v, lf_fw, s_fw)
    flip = lambda t: jnp.flip(t, axis=2)
    o_bw, s_bw = _chunk_scan(flip(q), flip(k_bw), flip(v), flip(lf_bw), s_bw)
    return o_fw + flip(o_bw), s_fw, s_bw


def _hgrn2_readout(o, og, g_onorm, w_a):
    bsz, _, seq_len, _ = o.shape
    o = o * lax.rsqrt(jnp.mean(o * o, axis=-1, keepdims=True) + EPS)
    o = o.transpose(0, 2, 1, 3).reshape(bsz, seq_len, D_A) * g_onorm
    return (o.astype(og.dtype) * jax.nn.silu(og)) @ w_a


def _conformer_conv(u, conv_w, conv_b, ln_g, ln_b, w_b):
    a, g = jnp.split(u, 2, axis=-1)
    h = a * jax.nn.sigmoid(g)
    h = lax.conv_general_dilated(h, conv_w.astype(h.dtype), window_strides=(1,),
                                 padding=[(CONV_PAD, CONV_PAD)],
                                 dimension_numbers=('NWC', 'WIO', 'NWC'),
                                 feature_group_count=D_B) + conv_b
    hf = h.astype(jnp.float32)
    mu = jnp.mean(hf, axis=-1, keepdims=True)
    var = jnp.mean(jnp.square(hf - mu), axis=-1, keepdims=True)
    hn = (hf - mu) * lax.rsqrt(var + EPS) * ln_g + ln_b
    return jax.nn.silu(hn).astype(u.dtype) @ w_b


def _merge(ya, yb, ga, gb, w_o):
    return (jax.nn.sigmoid(ga) * ya + jax.nn.sigmoid(gb) * yb) @ w_o


def _mixer(h, hc, w_in, lb, g_onorm, conv_w, conv_b, ln_g, ln_b, w_a, w_b, w_o, ctx_out):
    q, v, f_fw, f_bw, og, glu, ga, gb = _split_proj(h @ w_in)
    qc, vc, f_fwc, f_bwc, ogc, gluc, gac, gbc = _split_proj(hc @ w_in)
    s0 = jnp.zeros((h.shape[0], A_HEADS, A_HEAD_DIM, A_HEAD_DIM), jnp.float32)
    oc, s_fw, s_bw = _hgrn2_bidir(qc, vc, f_fwc, f_bwc, lb, s0, s0)
    o, _, _ = _hgrn2_bidir(q, v, f_fw, f_bw, lb, s_fw, s_bw)
    y = _merge(_hgrn2_readout(o, og, g_onorm, w_a),
               _conformer_conv(glu, conv_w, conv_b, ln_g, ln_b, w_b), ga, gb, w_o)
    if not ctx_out:
        return y, None
    yc = _merge(_hgrn2_readout(oc, ogc, g_onorm, w_a),
                _conformer_conv(gluc, conv_w, conv_b, ln_g, ln_b, w_b), gac, gbc, w_o)
    return y, yc


def _swiglu(h, w_gate, w_up, w_down):
    return (jax.nn.silu(h @ w_gate) * (h @ w_up)) @ w_down


def setup_inputs(seed: int = 0) -> dict:
    key = jax.random.key(seed)
    ks = jax.random.split(key, 24)
    f32 = jnp.float32
    nrm = lambda k, shape, s: jax.random.normal(k, shape, f32) * s
    return {
        'x': nrm(ks[0], (BATCH, SEQ, D_MODEL), 1.0),
        'c': nrm(ks[1], (BATCH, D_MODEL), 1.0),
        'ctx': nrm(ks[2], (BATCH, CTX_LEN, D_MODEL), 1.0),
        'c_ctx': nrm(ks[3], (D_MODEL,), 1.0),
        'w_mod': nrm(ks[4], (DEPTH, D_MODEL, MOD_DIM), 0.5 * D_MODEL ** -0.5),
        'b_mod': nrm(ks[5], (DEPTH, MOD_DIM), 0.01),
        'g_norm1': 1.0 + nrm(ks[6], (DEPTH, D_MODEL), 0.01),
        'w_in': nrm(ks[7], (DEPTH, D_MODEL, P_IN), D_MODEL ** -0.5),
        'lb_logits': nrm(ks[8], (DEPTH, 2, D_A), 1.0),
        'g_onorm': 1.0 + nrm(ks[9], (DEPTH, D_A), 0.01),
        'w_a': nrm(ks[10], (DEPTH, D_A, D_MODEL), D_A ** -0.5),
        'conv_w': nrm(ks[11], (DEPTH, CONV_K, 1, D_B), CONV_K ** -0.5),
        'conv_b': nrm(ks[12], (DEPTH, D_B), 0.01),
        'ln_g': 1.0 + nrm(ks[13], (DEPTH, D_B), 0.01),
        'ln_b': nrm(ks[14], (DEPTH, D_B), 0.01),
        'w_b': nrm(ks[15], (DEPTH, D_B, D_MODEL), D_B ** -0.5),
        'w_o': nrm(ks[16], (DEPTH, D_MODEL, D_MODEL), D_MODEL ** -0.5),
        'g_norm2': 1.0 + nrm(ks[17], (DEPTH, D_MODEL), 0.01),
        'w_ffn_gate': nrm(ks[18], (DEPTH, D_MODEL, D_FF), D_MODEL ** -0.5),
        'w_ffn_up': nrm(ks[19], (DEPTH, D_MODEL, D_FF), D_MODEL ** -0.5),
        'w_ffn_down': nrm(ks[20], (DEPTH, D_FF, D_MODEL), D_FF ** -0.5),
        'g_final': 1.0 + nrm(ks[21], (D_MODEL,), 0.01),
    }


def reference(x, c, ctx, c_ctx, w_mod, b_mod, g_norm1, w_in, lb_logits, g_onorm, w_a, conv_w,
              conv_b, ln_g, ln_b, w_b, w_o, g_norm2, w_ffn_gate, w_ffn_up, w_ffn_down, g_final):
    seq_len = x.shape[1]
    x = x + _grid_pos_embed(seq_len, x.shape[-1]).astype(x.dtype)
    lb_all = jnp.cumsum(jax.nn.softmax(lb_logits.astype(jnp.float32), axis=0), axis=0)
    lb_all = lb_all - lb_all[0:1]
    c_act = jax.nn.silu(c)
    cc_act = jax.nn.silu(c_ctx)
    for layer in range(DEPTH):
        last = layer == DEPTH - 1
        lb = lb_all[layer].reshape(2, A_HEADS, 1, A_HEAD_DIM)
        mod = c_act @ w_mod[layer] + b_mod[layer]
        mod_c = cc_act @ w_mod[layer] + b_mod[layer]
        sh1, sc1, gt1, sh2, sc2, gt2 = [m[:, None, :] for m in jnp.split(mod, 6, axis=-1)]
        sh1c, sc1c, gt1c, sh2c, sc2c, gt2c = jnp.split(mod_c, 6)
        h = _modulate(x, g_norm1[layer], sh1, sc1)
        hc = _modulate(ctx, g_norm1[layer], sh1c, sc1c)
        y, yc = _mixer(h, hc, w_in[layer], lb, g_onorm[layer], conv_w[layer], conv_b[layer],
                       ln_g[layer], ln_b[layer], w_a[layer], w_b[layer], w_o[layer], not last)
        x = x + gt1 * y
        h = _modulate(x, g_norm2[layer], sh2, sc2)
        x = x + gt2 * _swiglu(h, w_ffn_gate[layer], w_ffn_up[layer], w_ffn_down[layer])
        if not last:
            ctx = ctx + gt1c * yc
            hc = _modulate(ctx, g_norm2[layer], sh2c, sc2c)
            ctx = ctx + gt2c * _swiglu(hc, w_ffn_gate[layer], w_ffn_up[layer], w_ffn_down[layer])
    return _rmsnorm(x, g_final)
```

```python
import functools

import jax
import jax.numpy as jnp
from jax import lax
from jax.experimental import pallas as pl
from jax.experimental.pallas import tpu as pltpu

EPS = 1e-6
GRID_W = 64
POS_BASE = 10000.0
HEAD_DIM = 128
CHUNK = 64
HALF = CHUNK // 2
SAFE_LOG_DECAY = 60.0
CONV_HALO = 16
LANES = 128
MIB = 1 << 20
VMEM_CAP = 60 * MIB

F32 = jnp.float32
BF16 = jnp.bfloat16


def _tile(n, pref, mult):
    t = (min(pref, n) // mult) * mult
    while t >= mult:
        if n % t == 0:
            return t
        t -= mult
    raise ValueError(f"no tile for {n} (pref {pref}, mult {mult})")


def _params(sem, vmem_bytes):
    return pltpu.CompilerParams(dimension_semantics=sem,
                                vmem_limit_bytes=int(min(max(vmem_bytes, 16 * MIB), VMEM_CAP)))


def _sigmoid(x):
    return jax.nn.sigmoid(x)


def _silu(x):
    return x * jax.nn.sigmoid(x)


def _log_sigmoid(x):
    return jnp.minimum(x, 0.0) - jnp.log1p(jnp.exp(-jnp.abs(x)))


def _modvec_kernel(a_ref, w_ref, b_ref, o_ref):
    a = a_ref[...]
    act = (a * _sigmoid(a)).astype(BF16)
    o_ref[...] = jnp.dot(act, w_ref[...].astype(BF16), preferred_element_type=F32) + b_ref[...]


def _modvec(cond_rows, w_mod, b_mod):
    depth, d, md = w_mod.shape
    bn = _tile(md, 512, LANES)
    return pl.pallas_call(
        _modvec_kernel,
        grid=(depth, md // bn),
        in_specs=[pl.BlockSpec((8, d), lambda l, j: (0, 0)),
                  pl.BlockSpec((None, d, bn), lambda l, j: (l, 0, j)),
                  pl.BlockSpec((None, 1, bn), lambda l, j: (l, 0, j))],
        out_specs=pl.BlockSpec((None, 8, bn), lambda l, j: (l, 0, j)),
        out_shape=jax.ShapeDtypeStruct((depth, 8, md), F32),
        compiler_params=_params(("arbitrary", "arbitrary"), 2 * d * bn * 4 + d * bn * 2 + 4 * MIB),
        name="modvec",
    )(cond_rows, w_mod, b_mod.reshape(depth, 1, md))


def _modulate_kernel(*refs, with_pos, bm):
    if with_pos:
        x_ref, g_ref, sh_ref, sc_ref, prow_ref, pcol_ref, h_ref, xo_ref = refs
        half = x_ref.shape[1] // 2
        for r in range(bm // GRID_W):
            rows = slice(r * GRID_W, (r + 1) * GRID_W)
            xo_ref[rows, :half] = x_ref[rows, :half] + prow_ref[r:r + 1, :]
            xo_ref[rows, half:] = x_ref[rows, half:] + pcol_ref[...]
        x = xo_ref[...]
    else:
        x_ref, g_ref, sh_ref, sc_ref, h_ref = refs
        x = x_ref[...]
    ms = jnp.mean(x * x, axis=-1, keepdims=True)
    y = x * lax.rsqrt(ms + EPS) * g_ref[...]
    h_ref[...] = (y * (1.0 + sc_ref[...]) + sh_ref[...]).astype(h_ref.dtype)


def _modulate(x2d, gain, shift, scale, seq_len, pos=None):
    m, d = x2d.shape
    with_pos = pos is not None
    bm = _tile(seq_len, 256, GRID_W if with_pos else 8)
    tpb = seq_len // bm
    in_specs = [pl.BlockSpec((bm, d), lambda i: (i, 0)),
                pl.BlockSpec((1, d), lambda i: (0, 0)),
                pl.BlockSpec((None, 1, d), lambda i: (i // tpb, 0, 0)),
                pl.BlockSpec((None, 1, d), lambda i: (i // tpb, 0, 0))]
    args = [x2d, gain.reshape(1, d), shift, scale]
    out_specs = [pl.BlockSpec((bm, d), lambda i: (i, 0))]
    out_shape = [jax.ShapeDtypeStruct((m, d), BF16)]
    if with_pos:
        prow, pcol = pos
        rpt = bm // GRID_W
        in_specs += [pl.BlockSpec((None, rpt, d // 2), lambda i: (i % tpb, 0, 0)),
                     pl.BlockSpec((GRID_W, d // 2), lambda i: (0, 0))]
        args += [prow.reshape(tpb, rpt, d // 2), pcol]
        out_specs.append(pl.BlockSpec((bm, d), lambda i: (i, 0)))
        out_shape.append(jax.ShapeDtypeStruct((m, d), F32))
    out = pl.pallas_call(
        functools.partial(_modulate_kernel, with_pos=with_pos, bm=bm),
        grid=(m // bm,),
        in_specs=in_specs, out_specs=out_specs, out_shape=out_shape,
        compiler_params=_params(("arbitrary",), 2 * bm * d * (4 + 2 + (4 if with_pos else 0)) + 8 * MIB),
        name="modulate_pos" if with_pos else "modulate",
    )(*args)
    return (out[0], out[1]) if with_pos else out[0]


def _final_norm_kernel(x_ref, g_ref, o_ref):
    x = x_ref[...]
    ms = jnp.mean(x * x, axis=-1, keepdims=True)
    o_ref[...] = x * lax.rsqrt(ms + EPS) * g_ref[...]


def _final_norm(x2d, gain):
    m, d = x2d.shape
    bm = _tile(m, 256, 8)
    return pl.pallas_call(
        _final_norm_kernel,
        grid=(m // bm,),
        in_specs=[pl.BlockSpec((bm, d), lambda i: (i, 0)), pl.BlockSpec((1, d), lambda i: (0, 0))],
        out_specs=pl.BlockSpec((bm, d), lambda i: (i, 0)),
        out_shape=jax.ShapeDtypeStruct((m, d), F32),
        compiler_params=_params(("arbitrary",), 4 * bm * d * 4 + 8 * MIB),
        name="final_norm",
    )(x2d, gain.reshape(1, d))


def _matmul_kernel(*refs, n_a, dots, n_extra, nk, body):
    a_refs = refs[:n_a]
    w_refs = refs[n_a:n_a + len(dots)]
    e_refs = refs[n_a + len(dots):n_a + len(dots) + n_extra]
    rest = refs[n_a + len(dots) + n_extra:]
    if nk == 1:
        o_refs = rest
        accs = [jnp.dot(a_refs[ai][...], w_refs[d][...], preferred_element_type=F32)
                for d, ai in enumerate(dots)]
        body(accs, e_refs, o_refs)
        return
    o_refs, acc_refs = rest[:len(rest) - len(dots)], rest[len(rest) - len(dots):]
    k = pl.program_id(2)
    for d, ai in enumerate(dots):
        part = jnp.dot(a_refs[ai][...], w_refs[d][...], preferred_element_type=F32)

        @pl.when(k == 0)
        def _(part=part, d=d):
            acc_refs[d][...] = part

        @pl.when(k > 0)
        def _(part=part, d=d):
            acc_refs[d][...] += part

    @pl.when(k == nk - 1)
    def _():
        body([r[...] for r in acc_refs], e_refs, o_refs)


def _matmul(a_list, dots, extras, outs, body, *, bm, bn, n_col_blocks, bk=None, name):
    m, kdim = a_list[0].shape
    bk = kdim if bk is None else bk
    nk = kdim // bk
    assert m % bm == 0 and kdim % bk == 0
    if nk == 1:
        wrap = lambda f: (lambda i, j: f(i, j))
        a_map = lambda i, j: (i, 0)
        w_map = lambda cf: (lambda i, j: (0, cf(j)))
        grid = (m // bm, n_col_blocks)
        sem = ("arbitrary", "arbitrary")
    else:
        wrap = lambda f: (lambda i, j, k: f(i, j))
        a_map = lambda i, j, k: (i, k)
        w_map = lambda cf: (lambda i, j, k: (k, cf(j)))
        grid = (m // bm, n_col_blocks, nk)
        sem = ("arbitrary", "arbitrary", "arbitrary")
    a_k = [bk if nk > 1 else a.shape[1] for a in a_list]
    assert all(w.shape[0] == a_list[ai].shape[1] for (ai, w, _) in dots)
    in_specs = [pl.BlockSpec((bm, ak), a_map) for ak in a_k]
    in_specs += [pl.BlockSpec((a_k[ai], bn), w_map(cf)) for (ai, _, cf) in dots]
    in_specs += [pl.BlockSpec(bs, wrap(f)) for (_, bs, f) in extras]
    out_specs = [pl.BlockSpec(bs, wrap(f)) for (_, bs, f) in outs]
    scratch = [pltpu.VMEM((bm, bn), F32) for _ in dots] if nk > 1 else []

    def nbytes(shape, dtype):
        n = 1
        for s in shape:
            n *= (s or 1)
        return n * jnp.dtype(dtype).itemsize

    vmem = 2 * (sum(bm * ak * 2 for ak in a_k) + sum(a_k[ai] * bn * 2 for (ai, _, _) in dots))
    vmem += 2 * sum(nbytes(bs, arr.dtype) for (arr, bs, _) in extras)
    vmem += 2 * sum(nbytes(bs, sd.dtype) for (sd, bs, _) in outs)
    vmem += (2 * len(dots) + 1) * bm * bn * 4 + 4 * MIB
    return pl.pallas_call(
        functools.partial(_matmul_kernel, n_a=len(a_list), dots=[ai for (ai, _, _) in dots],
                          n_extra=len(extras), nk=nk, body=body),
        grid=grid, in_specs=in_specs, out_specs=out_specs,
        out_shape=[sd for (sd, _, _) in outs],
        scratch_shapes=scratch,
        compiler_params=_params(sem, vmem),
        name=name,
    )(*a_list, *[w for (_, w, _) in dots], *[arr for (arr, _, _) in extras])


def _inproj_main(h, w_in, d_a, d_b, d_model):
    m = h.shape[0]
    bm = _tile(m, 1024, 8)
    bn = _tile(d_a, 1024, LANES)
    groups = [(0, d_a, "silu"), (d_a, d_a, "id"), (4 * d_a, d_a, "silu"),
              (5 * d_a + 2 * d_b, 2 * d_model, "sigmoid")]
    ranges, j0 = [], 0
    for (src, width, act) in groups:
        assert src % bn == 0 and width % bn == 0
        ranges.append((j0, j0 + width // bn, src // bn - j0, act))
        j0 += width // bn
    n_blocks = j0

    def col_fn(j):
        blk = j + ranges[0][2]
        for (a0, _, off, _) in ranges[1:]:
            blk = jnp.where(j >= a0, j + off, blk)
        return blk

    acts = {"silu": _silu, "id": lambda x: x, "sigmoid": _sigmoid}

    def body(accs, e_refs, o_refs):
        j = pl.program_id(1)
        for (a0, a1, _, act) in ranges:
            @pl.when((j >= a0) & (j < a1))
            def _(act=act):
                o_refs[0][...] = acts[act](accs[0]).astype(BF16)

    n_out = n_blocks * bn
    return _matmul([h], [(0, w_in, col_fn)], [],
                   [(jax.ShapeDtypeStruct((m, n_out), BF16), (bm, bn), lambda i, j: (i, j))],
                   body, bm=bm, bn=bn, n_col_blocks=n_blocks, name="inproj_main")[0]


def _inproj_forget(h, w_in, lbvec, d_a):
    m = h.shape[0]
    bm = _tile(m, 1024, 8)
    bn = _tile(d_a, 512, LANES)
    off = (2 * d_a) // bn

    def body(accs, e_refs, o_refs):
        x = accs[0]
        log_lb, log_1m_lb, one_m_lb = e_refs[0][0], e_refs[0][1], e_refs[0][2]
        a = jnp.broadcast_to(log_lb, x.shape)
        b = log_1m_lb + _log_sigmoid(x)
        hi = jnp.maximum(a, b)
        o_refs[0][...] = hi + jnp.log1p(jnp.exp(-jnp.abs(a - b)))
        o_refs[1][...] = (one_m_lb * _sigmoid(-x)).astype(BF16)

    return _matmul([h], [(0, w_in, lambda j: j + off)],
                   [(lbvec, (3, 1, bn), lambda i, j: (0, 0, j))],
                   [(jax.ShapeDtypeStruct((m, 2 * d_a), F32), (bm, bn), lambda i, j: (i, j)),
                    (jax.ShapeDtypeStruct((m, 2 * d_a), BF16), (bm, bn), lambda i, j: (i, j))],
                   body, bm=bm, bn=bn, n_col_blocks=(2 * d_a) // bn, name="inproj_forget")


def _inproj_glu(h, w_in, d_a, d_b):
    m = h.shape[0]
    bm = _tile(m, 1024, 8)
    bn = _tile(d_b, 512, LANES)
    off_a, off_g = (5 * d_a) // bn, (5 * d_a + d_b) // bn

    def body(accs, e_refs, o_refs):
        o_refs[0][...] = (accs[0] * _sigmoid(accs[1])).astype(BF16)

    return _matmul([h], [(0, w_in, lambda j: j + off_a), (0, w_in, lambda j: j + off_g)], [],
                   [(jax.ShapeDtypeStruct((m, d_b), BF16), (bm, bn), lambda i, j: (i, j))],
                   body, bm=bm, bn=bn, n_col_blocks=d_b // bn, name="inproj_glu")[0]


def _merge(a_in, b_in, w_a, w_b, z1, gate_col0):
    m, d_model = a_in.shape[0], w_a.shape[1]
    bm = _tile(m, 1024, 8)
    bn = _tile(d_model, 512, LANES)
    ga0 = gate_col0 // bn
    gb0 = (gate_col0 + d_model) // bn

    def body(accs, e_refs, o_refs):
        ga = e_refs[0][...].astype(F32)
        gb = e_refs[1][...].astype(F32)
        o_refs[0][...] = (ga * accs[0] + gb * accs[1]).astype(BF16)

    return _matmul([a_in, b_in], [(0, w_a, lambda j: j), (1, w_b, lambda j: j)],
                   [(z1, (bm, bn), lambda i, j: (i, j + ga0)), (z1, (bm, bn), lambda i, j: (i, j + gb0))],
                   [(jax.ShapeDtypeStruct((m, d_model), BF16), (bm, bn), lambda i, j: (i, j))],
                   body, bm=bm, bn=bn, n_col_blocks=d_model // bn, name="merge")[0]


def _proj_residual(a, w, x2d, gate, seq_len, *, bk=None, name):
    m, d_model = x2d.shape
    bm = _tile(seq_len, 1024, 8)
    bn = _tile(d_model, 512, LANES)
    tpb = seq_len // bm

    def body(accs, e_refs, o_refs):
        o_refs[0][...] = e_refs[0][...] + e_refs[1][...] * accs[0]

    return _matmul([a], [(0, w, lambda j: j)],
                   [(x2d, (bm, bn), lambda i, j: (i, j)), (gate, (None, 1, bn), lambda i, j: (i // tpb, 0, j))],
                   [(jax.ShapeDtypeStruct((m, d_model), F32), (bm, bn), lambda i, j: (i, j))],
                   body, bm=bm, bn=bn, n_col_blocks=d_model // bn, bk=bk, name=name)[0]


def _ffn_up(h, w_gate, w_up):
    m, d_ff = h.shape[0], w_gate.shape[1]
    bm = _tile(m, 1024, 8)
    bn = _tile(d_ff, 512, LANES)

    def body(accs, e_refs, o_refs):
        o_refs[0][...] = (_silu(accs[0]) * accs[1]).astype(BF16)

    return _matmul([h], [(0, w_gate, lambda j: j), (0, w_up, lambda j: j)], [],
                   [(jax.ShapeDtypeStruct((m, d_ff), BF16), (bm, bn), lambda i, j: (i, j))],
                   body, bm=bm, bn=bn, n_col_blocks=d_ff // bn, name="ffn_up")[0]


def _cumsum_rows(x, reverse):
    n = x.shape[0]
    rows = lax.broadcasted_iota(jnp.int32, x.shape, 0)
    s = 1
    while s < n:
        if reverse:
            x = x + jnp.where(rows < n - s, pltpu.roll(x, n - s, 0), 0.0)
        else:
            x = x + jnp.where(rows >= s, pltpu.roll(x, s, 0), 0.0)
        s *= 2
    return x


_NT = (((1,), (1,)), ((), ()))
_TN = (((0,), (0,)), ((), ()))


def _scan_kernel(*refs, reverse, finalize, tb, hb):
    if finalize:
        (q_ref, v_ref, g_ref, k_ref, s0_ref, ofw_ref, sog_ref, gon_ref,
         o_ref, sfin_ref, st_ref) = refs
    else:
        (q_ref, v_ref, g_ref, k_ref, s0_ref,
         o_ref, sfin_ref, st_ref) = refs
    step = pl.program_id(2)
    n_chunks = tb // CHUNK

    @pl.when(step == 0)
    def _():
        st_ref[...] = s0_ref[...]

    half_sums = jnp.sum(g_ref[...].reshape(tb // HALF, HALF, hb * HEAD_DIM), axis=1)
    safe = jnp.min(half_sums) > -SAFE_LOG_DECAY

    tri_r = lax.broadcasted_iota(jnp.int32, (CHUNK, CHUNK), 0)
    tri_c = lax.broadcasted_iota(jnp.int32, (CHUNK, CHUNK), 1)
    causal = (tri_c >= tri_r) if reverse else (tri_c <= tri_r)
    ref_row = HALF if reverse else HALF - 1
    end_row = 0 if reverse else CHUNK - 1

    def emit(rows, lanes, hh, o):
        if finalize:
            tot = o + ofw_ref[rows, lanes]
            ms = jnp.mean(tot * tot, axis=-1, keepdims=True)
            y = tot * lax.rsqrt(ms + EPS) * gon_ref[:, lanes]
            o_ref[rows, lanes] = (y * sog_ref[rows, lanes].astype(F32)).astype(o_ref.dtype)
        else:
            o_ref[rows, lanes] = o

    def instance(c, hh, fast):
        rows = slice(c * CHUNK, (c + 1) * CHUNK)
        lanes = slice(hh * HEAD_DIM, (hh + 1) * HEAD_DIM)
        b = _cumsum_rows(g_ref[rows, lanes], reverse)
        q = q_ref[rows, lanes].astype(F32)
        k = k_ref[rows, lanes].astype(F32)
        v = v_ref[rows, lanes]
        st = st_ref[hh]
        b_end = b[end_row:end_row + 1, :]
        if fast:
            r = b[ref_row:ref_row + 1, :]
            qd = (q * jnp.exp(b - r)).astype(BF16)
            kd = (k * jnp.exp(r - b)).astype(BF16)
            scores = lax.dot_general(qd, kd, _NT, preferred_element_type=F32)
            o_int = lax.dot_general(qd, (st * jnp.exp(r)).astype(BF16), _NT, preferred_element_type=F32)
            upd = lax.dot_general(v, kd, _TN, preferred_element_type=F32) * jnp.exp(b_end - r)
        else:
            row_id = lax.broadcasted_iota(jnp.int32, b.shape, 0)

            def col(s, sc):
                pick = row_id == s
                b_s = jnp.sum(jnp.where(pick, b, 0.0), axis=0, keepdims=True)
                k_s = jnp.sum(jnp.where(pick, k, 0.0), axis=0, keepdims=True)
                d = jnp.exp(jnp.minimum(b - b_s, 0.0))
                cs = jnp.sum(q * d * k_s, axis=-1, keepdims=True)
                return jnp.where(tri_c == s, cs, sc)

            scores = lax.fori_loop(0, CHUNK, col, jnp.zeros((CHUNK, CHUNK), F32))
            o_int = lax.dot_general((q * jnp.exp(b)).astype(BF16), st.astype(BF16), _NT,
                                    preferred_element_type=F32)
            upd = lax.dot_general(v, (k * jnp.exp(b_end - b)).astype(BF16), _TN,
                                  preferred_element_type=F32)
        p = jnp.where(causal, scores, 0.0).astype(BF16)
        o = o_int + jnp.dot(p, v, preferred_element_type=F32)
        st_ref[hh] = st * jnp.exp(b_end) + upd
        emit(rows, lanes, hh, o)

    def run(fast):
        order = range(n_chunks - 1, -1, -1) if reverse else range(n_chunks)
        for c in order:
            for hh in range(hb):
                instance(c, hh, fast)

    @pl.when(safe)
    def _():
        run(True)

    @pl.when(jnp.logical_not(safe))
    def _():
        run(False)

    @pl.when(step == pl.num_programs(2) - 1)
    def _():
        sfin_ref[...] = st_ref[...]


def _scan(z1, gf, kf, s0, nb, seq_len, d_a, *, reverse, fin=None):
    heads = d_a // HEAD_DIM
    hb = 4 if heads % 4 == 0 else 1
    width = hb * HEAD_DIM
    tb = _tile(seq_len, 256, CHUNK)
    nt = seq_len // tb
    ng = heads // hb
    gcols = d_a // width
    dirn = 1 if reverse else 0

    def row(b, i):
        return b * nt + ((nt - 1 - i) if reverse else i)

    in_specs = [pl.BlockSpec((tb, width), lambda b, h, i: (row(b, i), h)),
                pl.BlockSpec((tb, width), lambda b, h, i: (row(b, i), gcols + h)),
                pl.BlockSpec((tb, width), lambda b, h, i: (row(b, i), dirn * gcols + h)),
                pl.BlockSpec((tb, width), lambda b, h, i: (row(b, i), dirn * gcols + h)),
                pl.BlockSpec((None, hb, HEAD_DIM, HEAD_DIM), lambda b, h, i: (b, h, 0, 0))]
    args = [z1, z1, gf, kf, s0]
    if fin is not None:
        o_fw, g_onorm = fin
        in_specs += [pl.BlockSpec((tb, width), lambda b, h, i: (row(b, i), h)),
                     pl.BlockSpec((tb, width), lambda b, h, i: (row(b, i), 2 * gcols + h)),
                     pl.BlockSpec((1, width), lambda b, h, i: (0, h))]
        args += [o_fw, z1, g_onorm.reshape(1, d_a)]
    out_dtype = BF16 if fin is not None else F32
    out_specs = [pl.BlockSpec((tb, width), lambda b, h, i: (row(b, i), h)),
                 pl.BlockSpec((None, hb, HEAD_DIM, HEAD_DIM), lambda b, h, i: (b, h, 0, 0))]
    out_shape = [jax.ShapeDtypeStruct((nb * seq_len, d_a), out_dtype),
                 jax.ShapeDtypeStruct((nb, heads, HEAD_DIM, HEAD_DIM), F32)]
    return pl.pallas_call(
        functools.partial(_scan_kernel, reverse=reverse, finalize=fin is not None, tb=tb, hb=hb),
        grid=(nb, ng, nt),
        in_specs=in_specs, out_specs=out_specs, out_shape=out_shape,
        scratch_shapes=[pltpu.VMEM((hb, HEAD_DIM, HEAD_DIM), F32)],
        compiler_params=_params(("arbitrary", "arbitrary", "arbitrary"), 32 * MIB),
        name="scan_bw" if reverse else "scan_fw",
    )(*args)


def _conv_kernel(cur_ref, prev_ref, next_ref, w_ref, cb_ref, lg_ref, lb_ref, o_ref, ext_ref, acc_ref,
                 *, tb, taps):
    i = pl.program_id(1)
    nt = pl.num_programs(1)
    pad = (taps - 1) // 2
    d_b = cur_ref.shape[1]
    n_lt = d_b // LANES
    zeros = jnp.zeros((CONV_HALO, LANES), F32)
    for lt in range(n_lt):
        lanes = slice(lt * LANES, (lt + 1) * LANES)
        ext_ref[lt, 0:CONV_HALO, :] = jnp.where(i > 0, prev_ref[:, lanes].astype(F32), zeros)
        ext_ref[lt, CONV_HALO:CONV_HALO + tb, :] = cur_ref[:, lanes].astype(F32)
        ext_ref[lt, CONV_HALO + tb:, :] = jnp.where(i < nt - 1, next_ref[:, lanes].astype(F32), zeros)
    rg = 64

    def lane_tile(lt, carry):
        for g in range(tb // rg):
            base = g * rg + CONV_HALO - pad
            acc = jnp.zeros((rg, LANES), F32)
            for j in range(taps):
                acc = acc + w_ref[lt, j:j + 1, :] * ext_ref[lt, base + j:base + j + rg, :]
            acc_ref[lt, g * rg:(g + 1) * rg, :] = acc
        return carry

    lax.fori_loop(0, n_lt, lane_tile, 0)
    tiles = [acc_ref[lt] + cb_ref[:, lt * LANES:(lt + 1) * LANES] for lt in range(n_lt)]
    mu = sum(jnp.sum(t, axis=-1, keepdims=True) for t in tiles) * (1.0 / d_b)
    cen = [t - mu for t in tiles]
    var = sum(jnp.sum(c * c, axis=-1, keepdims=True) for c in cen) * (1.0 / d_b)
    inv = lax.rsqrt(var + EPS)
    for lt in range(n_lt):
        lanes = slice(lt * LANES, (lt + 1) * LANES)
        hn = cen[lt] * inv * lg_ref[:, lanes] + lb_ref[:, lanes]
        o_ref[:, lanes] = _silu(hn).astype(o_ref.dtype)


def _conv_module(hglu, conv_w, conv_b, ln_g, ln_b, nb, seq_len):
    m, d_b = hglu.shape
    taps = conv_w.shape[0]
    assert (taps - 1) // 2 < CONV_HALO
    tb = _tile(seq_len, 256, 64)
    nt = seq_len // tb
    hpb = tb // CONV_HALO
    nh = seq_len // CONV_HALO

    def prev_map(b, i):
        return (b * nh + jnp.maximum(i * hpb - 1, 0), 0)

    def next_map(b, i):
        return (b * nh + jnp.minimum((i + 1) * hpb, nh - 1), 0)

    vec = lambda: pl.BlockSpec((1, d_b), lambda b, i: (0, 0))
    return pl.pallas_call(
        functools.partial(_conv_kernel, tb=tb, taps=taps),
        grid=(nb, nt),
        in_specs=[pl.BlockSpec((tb, d_b), lambda b, i: (b * nt + i, 0)),
                  pl.BlockSpec((CONV_HALO, d_b), prev_map),
                  pl.BlockSpec((CONV_HALO, d_b), next_map),
                  pl.BlockSpec((d_b // LANES, taps, LANES), lambda b, i: (0, 0, 0)),
                  vec(), vec(), vec()],
        out_specs=pl.BlockSpec((tb, d_b), lambda b, i: (b * nt + i, 0)),
        out_shape=jax.ShapeDtypeStruct((m, d_b), BF16),
        scratch_shapes=[pltpu.VMEM((d_b // LANES, tb + 2 * CONV_HALO, LANES), F32),
                        pltpu.VMEM((d_b // LANES, tb, LANES), F32)],
        compiler_params=_params(("arbitrary", "arbitrary"), 12 * tb * d_b * 4 + 8 * MIB),
        name="conv_module",
    )(hglu, hglu, hglu, conv_w.reshape(taps, d_b // LANES, LANES).transpose(1, 0, 2), conv_b.reshape(1, d_b),
      ln_g.reshape(1, d_b), ln_b.reshape(1, d_b))


def _pos_tables(seq_len, dim):
    rows = seq_len // GRID_W
    quarter = dim // 4
    omega = 1.0 / (POS_BASE ** (jnp.arange(quarter, dtype=F32) / quarter))
    ang_r = jnp.arange(rows, dtype=F32)[:, None] * omega
    ang_c = jnp.arange(GRID_W, dtype=F32)[:, None] * omega
    prow = jnp.concatenate([jnp.sin(ang_r), jnp.cos(ang_r)], axis=-1)
    pcol = jnp.concatenate([jnp.sin(ang_c), jnp.cos(ang_c)], axis=-1)
    return prow, pcol


def _mixer_inputs(h, w_in, lbvec, d_a, d_b, d_model):
    z1 = _inproj_main(h, w_in, d_a, d_b, d_model)
    gf, kf = _inproj_forget(h, w_in, lbvec, d_a)
    return z1, gf, kf


def kernel(x, c, ctx, c_ctx, w_mod, b_mod, g_norm1, w_in, lb_logits, g_onorm, w_a, conv_w, conv_b,
           ln_g, ln_b, w_b, w_o, g_norm2, w_ffn_gate, w_ffn_up, w_ffn_down, g_final):
    bsz, seq_len, d_model = x.shape
    ctx_len = ctx.shape[1]
    depth = w_mod.shape[0]
    d_a = lb_logits.shape[-1]
    d_b = conv_b.shape[-1]
    d_ff = w_ffn_gate.shape[-1]
    heads = d_a // HEAD_DIM
    assert bsz + 1 <= 8

    lb_all = jnp.cumsum(jax.nn.softmax(lb_logits.astype(F32), axis=0), axis=0)
    lb_all = lb_all - lb_all[0:1]
    lb_flat = lb_all.reshape(depth, 1, 2 * d_a)
    lbvecs = jnp.stack([jnp.log(lb_flat), jnp.log1p(-lb_flat), 1.0 - lb_flat], axis=1)

    cond = jnp.zeros((8, d_model), F32).at[:bsz].set(c).at[bsz].set(c_ctx)
    mod = _modvec(cond, w_mod, b_mod)

    ff_pad = (-d_ff) % 1024
    wb = lambda w: w.astype(BF16)
    w_in_b, w_a_b, w_b_b, w_o_b = wb(w_in), wb(w_a), wb(w_b), wb(w_o)
    w_gate_b = jnp.pad(wb(w_ffn_gate), ((0, 0), (0, 0), (0, ff_pad)))
    w_up_b = jnp.pad(wb(w_ffn_up), ((0, 0), (0, 0), (0, ff_pad)))
    w_down_b = jnp.pad(wb(w_ffn_down), ((0, 0), (0, ff_pad), (0, 0)))
    d_ffp = d_ff + ff_pad
    bk_down = _tile(d_ffp, 4096, LANES)

    pos = _pos_tables(seq_len, d_model)
    xl = x.reshape(bsz * seq_len, d_model)
    xc = ctx.reshape(bsz * ctx_len, d_model)
    s_zero = jnp.zeros((bsz, heads, HEAD_DIM, HEAD_DIM), F32)
    gate_col0 = 3 * d_a

    for layer in range(depth):
        last = layer == depth - 1
        mods = mod[layer].reshape(8, 6, d_model)
        lat = [mods[:bsz, k][:, None, :] for k in range(6)]
        cm = [jnp.broadcast_to(mods[bsz, k][None, None, :], (bsz, 1, d_model)) for k in range(6)]
        sh1, sc1, gt1, sh2, sc2, gt2 = lat
        sh1c, sc1c, gt1c, sh2c, sc2c, gt2c = cm

        if layer == 0:
            h, xl = _modulate(xl, g_norm1[layer], sh1, sc1, seq_len, pos=pos)
        else:
            h = _modulate(xl, g_norm1[layer], sh1, sc1, seq_len)
        hc = _modulate(xc, g_norm1[layer], sh1c, sc1c, ctx_len)

        z1c, gfc, kfc = _mixer_inputs(hc, w_in_b[layer], lbvecs[layer], d_a, d_b, d_model)
        oc_fw, s_fw = _scan(z1c, gfc, kfc, s_zero, bsz, ctx_len, d_a, reverse=False)
        ac, s_bw = _scan(z1c, gfc, kfc, s_zero, bsz, ctx_len, d_a, reverse=True,
                         fin=(oc_fw, g_onorm[layer]))

        z1, gf, kf = _mixer_inputs(h, w_in_b[layer], lbvecs[layer], d_a, d_b, d_model)
        o_fw, _ = _scan(z1, gf, kf, s_fw, bsz, seq_len, d_a, reverse=False)
        a_in, _ = _scan(z1, gf, kf, s_bw, bsz, seq_len, d_a, reverse=True, fin=(o_fw, g_onorm[layer]))

        hglu = _inproj_glu(h, w_in_b[layer], d_a, d_b)
        b_in = _conv_module(hglu, conv_w[layer], conv_b[layer], ln_g[layer], ln_b[layer], bsz, seq_len)
        mrg = _merge(a_in, b_in, w_a_b[layer], w_b_b[layer], z1, gate_col0)
        xl = _proj_residual(mrg, w_o_b[layer], xl, gt1, seq_len, name="out_proj")
        h2 = _modulate(xl, g_norm2[layer], sh2, sc2, seq_len)
        u = _ffn_up(h2, w_gate_b[layer], w_up_b[layer])
        xl = _proj_residual(u, w_down_b[layer], xl, gt2, seq_len, bk=bk_down, name="ffn_down")

        if not last:
            hgluc = _inproj_glu(hc, w_in_b[layer], d_a, d_b)
            bc = _conv_module(hgluc, conv_w[layer], conv_b[layer], ln_g[layer], ln_b[layer], bsz, ctx_len)
            mrgc = _merge(ac, bc, w_a_b[layer], w_b_b[layer], z1c, gate_col0)
            xc = _proj_residual(mrgc, w_o_b[layer], xc, gt1c, ctx_len, name="out_proj_ctx")
            h2c = _modulate(xc, g_norm2[layer], sh2c, sc2c, ctx_len)
            uc = _ffn_up(h2c, w_gate_b[layer], w_up_b[layer])
            xc = _proj_residual(uc, w_down_b[layer], xc, gt2c, ctx_len, bk=bk_down, name="ffn_down_ctx")

    return _final_norm(xl, g_final).reshape(bsz, seq_len, d_model)
```

```python
import functools

import jax
import jax.numpy as jnp
from jax import lax
from jax.experimental import pallas as pl
from jax.experimental.pallas import tpu as pltpu

EPS = 1e-6
GRID_W = 64
POS_BASE = 10000.0
HEAD_DIM = 128
CHUNK = 64
HALF = CHUNK // 2
SUB = 16
SAFE_STEP_LOG_DECAY = 75.0 / (SUB // 2)
CONV_HALO = 16
LANES = 128
MIB = 1 << 20
VMEM_CAP = 60 * MIB

F32 = jnp.float32
BF16 = jnp.bfloat16


def _tile(n, pref, mult):
    t = (min(pref, n) // mult) * mult
    while t >= mult:
        if n % t == 0:
            return t
        t -= mult
    raise ValueError(f"no tile for {n} (pref {pref}, mult {mult})")


def _params(sem, vmem_bytes):
    return pltpu.CompilerParams(dimension_semantics=sem,
                                vmem_limit_bytes=int(min(max(vmem_bytes, 16 * MIB), VMEM_CAP)))


def _sigmoid(x):
    return jax.nn.sigmoid(x)


def _silu(x):
    return x * jax.nn.sigmoid(x)


def _log_sigmoid(x):
    return jnp.minimum(x, 0.0) - jnp.log1p(jnp.exp(-jnp.abs(x)))


def _modvec_kernel(a_ref, w_ref, b_ref, o_ref):
    a = a_ref[...]
    act = (a * _sigmoid(a)).astype(BF16)
    o_ref[...] = jnp.dot(act, w_ref[...].astype(BF16), preferred_element_type=F32) + b_ref[...]


def _modvec(cond_rows, w_mod, b_mod):
    depth, d, md = w_mod.shape
    bn = _tile(md, 512, LANES)
    return pl.pallas_call(
        _modvec_kernel,
        grid=(depth, md // bn),
        in_specs=[pl.BlockSpec((8, d), lambda l, j: (0, 0)),
                  pl.BlockSpec((None, d, bn), lambda l, j: (l, 0, j)),
                  pl.BlockSpec((None, 1, bn), lambda l, j: (l, 0, j))],
        out_specs=pl.BlockSpec((None, 8, bn), lambda l, j: (l, 0, j)),
        out_shape=jax.ShapeDtypeStruct((depth, 8, md), F32),
        compiler_params=_params(("arbitrary", "arbitrary"), 2 * d * bn * 4 + d * bn * 2 + 4 * MIB),
        name="modvec",
    )(cond_rows, w_mod, b_mod.reshape(depth, 1, md))


def _modulate_kernel(*refs, with_pos, bm):
    if with_pos:
        x_ref, g_ref, sh_ref, sc_ref, prow_ref, pcol_ref, h_ref, xo_ref = refs
        half = x_ref.shape[1] // 2
        for r in range(bm // GRID_W):
            rows = slice(r * GRID_W, (r + 1) * GRID_W)
            xo_ref[rows, :half] = x_ref[rows, :half] + prow_ref[r:r + 1, :]
            xo_ref[rows, half:] = x_ref[rows, half:] + pcol_ref[...]
        x = xo_ref[...]
    else:
        x_ref, g_ref, sh_ref, sc_ref, h_ref = refs
        x = x_ref[...]
    ms = jnp.mean(x * x, axis=-1, keepdims=True)
    y = x * lax.rsqrt(ms + EPS) * g_ref[...]
    h_ref[...] = (y * (1.0 + sc_ref[...]) + sh_ref[...]).astype(h_ref.dtype)


def _modulate(x2d, gain, shift, scale, seq_len, pos=None):
    m, d = x2d.shape
    with_pos = pos is not None
    bm = _tile(seq_len, 256, GRID_W if with_pos else 8)
    tpb = seq_len // bm
    in_specs = [pl.BlockSpec((bm, d), lambda i: (i, 0)),
                pl.BlockSpec((1, d), lambda i: (0, 0)),
                pl.BlockSpec((None, 1, d), lambda i: (i // tpb, 0, 0)),
                pl.BlockSpec((None, 1, d), lambda i: (i // tpb, 0, 0))]
    args = [x2d, gain.reshape(1, d), shift, scale]
    out_specs = [pl.BlockSpec((bm, d), lambda i: (i, 0))]
    out_shape = [jax.ShapeDtypeStruct((m, d), BF16)]
    if with_pos:
        prow, pcol = pos
        rpt = bm // GRID_W
        in_specs += [pl.BlockSpec((None, rpt, d // 2), lambda i: (i % tpb, 0, 0)),
                     pl.BlockSpec((GRID_W, d // 2), lambda i: (0, 0))]
        args += [prow.reshape(tpb, rpt, d // 2), pcol]
        out_specs.append(pl.BlockSpec((bm, d), lambda i: (i, 0)))
        out_shape.append(jax.ShapeDtypeStruct((m, d), F32))
    out = pl.pallas_call(
        functools.partial(_modulate_kernel, with_pos=with_pos, bm=bm),
        grid=(m // bm,),
        in_specs=in_specs, out_specs=out_specs, out_shape=out_shape,
        compiler_params=_params(("arbitrary",), 2 * bm * d * (4 + 2 + (4 if with_pos else 0)) + 8 * MIB),
        name="modulate_pos" if with_pos else "modulate",
    )(*args)
    return (out[0], out[1]) if with_pos else out[0]


def _final_norm_kernel(x_ref, g_ref, o_ref):
    x = x_ref[...]
    ms = jnp.mean(x * x, axis=-1, keepdims=True)
    o_ref[...] = x * lax.rsqrt(ms + EPS) * g_ref[...]


def _final_norm(x2d, gain):
    m, d = x2d.shape
    bm = _tile(m, 256, 8)
    return pl.pallas_call(
        _final_norm_kernel,
        grid=(m // bm,),
        in_specs=[pl.BlockSpec((bm, d), lambda i: (i, 0)), pl.BlockSpec((1, d), lambda i: (0, 0))],
        out_specs=pl.BlockSpec((bm, d), lambda i: (i, 0)),
        out_shape=jax.ShapeDtypeStruct((m, d), F32),
        compiler_params=_params(("arbitrary",), 4 * bm * d * 4 + 8 * MIB),
        name="final_norm",
    )(x2d, gain.reshape(1, d))


def _matmul_kernel(*refs, n_a, dots, n_extra, nk, body):
    a_refs = refs[:n_a]
    w_refs = refs[n_a:n_a + len(dots)]
    e_refs = refs[n_a + len(dots):n_a + len(dots) + n_extra]
    rest = refs[n_a + len(dots) + n_extra:]
    if nk == 1:
        o_refs = rest
        accs = [jnp.dot(a_refs[ai][...], w_refs[d][...], preferred_element_type=F32)
                for d, ai in enumerate(dots)]
        body(accs, e_refs, o_refs)
        return
    o_refs, acc_refs = rest[:len(rest) - len(dots)], rest[len(rest) - len(dots):]
    k = pl.program_id(2)
    for d, ai in enumerate(dots):
        part = jnp.dot(a_refs[ai][...], w_refs[d][...], preferred_element_type=F32)

        @pl.when(k == 0)
        def _(part=part, d=d):
            acc_refs[d][...] = part

        @pl.when(k > 0)
        def _(part=part, d=d):
            acc_refs[d][...] += part

    @pl.when(k == nk - 1)
    def _():
        body([r[...] for r in acc_refs], e_refs, o_refs)


def _matmul(a_list, dots, extras, outs, body, *, bm, bn, n_col_blocks, bk=None, name):
    m, kdim = a_list[0].shape
    bk = kdim if bk is None else bk
    nk = kdim // bk
    assert m % bm == 0 and kdim % bk == 0
    if nk == 1:
        wrap = lambda f: (lambda i, j: f(i, j))
        a_map = lambda i, j: (i, 0)
        w_map = lambda cf: (lambda i, j: (0, cf(j)))
        grid = (m // bm, n_col_blocks)
        sem = ("arbitrary", "arbitrary")
    else:
        wrap = lambda f: (lambda i, j, k: f(i, j))
        a_map = lambda i, j, k: (i, k)
        w_map = lambda cf: (lambda i, j, k: (k, cf(j)))
        grid = (m // bm, n_col_blocks, nk)
        sem = ("arbitrary", "arbitrary", "arbitrary")
    a_k = [bk if nk > 1 else a.shape[1] for a in a_list]
    assert all(w.shape[0] == a_list[ai].shape[1] for (ai, w, _) in dots)
    in_specs = [pl.BlockSpec((bm, ak), a_map) for ak in a_k]
    in_specs += [pl.BlockSpec((a_k[ai], bn), w_map(cf)) for (ai, _, cf) in dots]
    in_specs += [pl.BlockSpec(bs, wrap(f)) for (_, bs, f) in extras]
    out_specs = [pl.BlockSpec(bs, wrap(f)) for (_, bs, f) in outs]
    scratch = [pltpu.VMEM((bm, bn), F32) for _ in dots] if nk > 1 else []

    def nbytes(shape, dtype):
        n = 1
        for s in shape:
            n *= (s or 1)
        return n * jnp.dtype(dtype).itemsize

    vmem = 2 * (sum(bm * ak * 2 for ak in a_k) + sum(a_k[ai] * bn * 2 for (ai, _, _) in dots))
    vmem += 2 * sum(nbytes(bs, arr.dtype) for (arr, bs, _) in extras)
    vmem += 2 * sum(nbytes(bs, sd.dtype) for (sd, bs, _) in outs)
    vmem += (2 * len(dots) + 1) * bm * bn * 4 + 4 * MIB
    return pl.pallas_call(
        functools.partial(_matmul_kernel, n_a=len(a_list), dots=[ai for (ai, _, _) in dots],
                          n_extra=len(extras), nk=nk, body=body),
        grid=grid, in_specs=in_specs, out_specs=out_specs,
        out_shape=[sd for (sd, _, _) in outs],
        scratch_shapes=scratch,
        compiler_params=_params(sem, vmem),
        name=name,
    )(*a_list, *[w for (_, w, _) in dots], *[arr for (arr, _, _) in extras])


def _inproj_main(h, w_in, d_a, d_b, d_model):
    m = h.shape[0]
    bm = _tile(m, 1024, 8)
    bn = _tile(d_a, 1024, LANES)
    groups = [(0, d_a, "silu"), (d_a, d_a, "id"), (4 * d_a, d_a, "silu"),
              (5 * d_a + 2 * d_b, 2 * d_model, "sigmoid")]
    ranges, j0 = [], 0
    for (src, width, act) in groups:
        assert src % bn == 0 and width % bn == 0
        ranges.append((j0, j0 + width // bn, src // bn - j0, act))
        j0 += width // bn
    n_blocks = j0

    def col_fn(j):
        blk = j + ranges[0][2]
        for (a0, _, off, _) in ranges[1:]:
            blk = jnp.where(j >= a0, j + off, blk)
        return blk

    acts = {"silu": _silu, "id": lambda x: x, "sigmoid": _sigmoid}

    def body(accs, e_refs, o_refs):
        j = pl.program_id(1)
        for (a0, a1, _, act) in ranges:
            @pl.when((j >= a0) & (j < a1))
            def _(act=act):
                o_refs[0][...] = acts[act](accs[0]).astype(BF16)

    n_out = n_blocks * bn
    return _matmul([h], [(0, w_in, col_fn)], [],
                   [(jax.ShapeDtypeStruct((m, n_out), BF16), (bm, bn), lambda i, j: (i, j))],
                   body, bm=bm, bn=bn, n_col_blocks=n_blocks, name="inproj_main")[0]


def _inproj_forget(h, w_in, lbvec, d_a):
    m = h.shape[0]
    bm = _tile(m, 1024, 8)
    bn = _tile(d_a, 512, LANES)
    off = (2 * d_a) // bn

    def body(accs, e_refs, o_refs):
        x = accs[0]
        log_lb, log_1m_lb, one_m_lb = e_refs[0][0], e_refs[0][1], e_refs[0][2]
        a = jnp.broadcast_to(log_lb, x.shape)
        b = log_1m_lb + _log_sigmoid(x)
        hi = jnp.maximum(a, b)
        o_refs[0][...] = hi + jnp.log1p(jnp.exp(-jnp.abs(a - b)))
        o_refs[1][...] = (one_m_lb * _sigmoid(-x)).astype(BF16)

    return _matmul([h], [(0, w_in, lambda j: j + off)],
                   [(lbvec, (3, 1, bn), lambda i, j: (0, 0, j))],
                   [(jax.ShapeDtypeStruct((m, 2 * d_a), F32), (bm, bn), lambda i, j: (i, j)),
                    (jax.ShapeDtypeStruct((m, 2 * d_a), BF16), (bm, bn), lambda i, j: (i, j))],
                   body, bm=bm, bn=bn, n_col_blocks=(2 * d_a) // bn, name="inproj_forget")


def _inproj_glu(h, w_in, d_a, d_b):
    m = h.shape[0]
    bm = _tile(m, 1024, 8)
    bn = _tile(d_b, 512, LANES)
    off_a, off_g = (5 * d_a) // bn, (5 * d_a + d_b) // bn

    def body(accs, e_refs, o_refs):
        o_refs[0][...] = (accs[0] * _sigmoid(accs[1])).astype(BF16)

    return _matmul([h], [(0, w_in, lambda j: j + off_a), (0, w_in, lambda j: j + off_g)], [],
                   [(jax.ShapeDtypeStruct((m, d_b), BF16), (bm, bn), lambda i, j: (i, j))],
                   body, bm=bm, bn=bn, n_col_blocks=d_b // bn, name="inproj_glu")[0]


def _merge(a_in, b_in, w_a, w_b, z1, gate_col0):
    m, d_model = a_in.shape[0], w_a.shape[1]
    bm = _tile(m, 1024, 8)
    bn = _tile(d_model, 512, LANES)
    ga0 = gate_col0 // bn
    gb0 = (gate_col0 + d_model) // bn

    def body(accs, e_refs, o_refs):
        ga = e_refs[0][...].astype(F32)
        gb = e_refs[1][...].astype(F32)
        o_refs[0][...] = (ga * accs[0] + gb * accs[1]).astype(BF16)

    return _matmul([a_in, b_in], [(0, w_a, lambda j: j), (1, w_b, lambda j: j)],
                   [(z1, (bm, bn), lambda i, j: (i, j + ga0)), (z1, (bm, bn), lambda i, j: (i, j + gb0))],
                   [(jax.ShapeDtypeStruct((m, d_model), BF16), (bm, bn), lambda i, j: (i, j))],
                   body, bm=bm, bn=bn, n_col_blocks=d_model // bn, name="merge")[0]


def _proj_residual(a, w, x2d, gate, seq_len, *, bk=None, name):
    m, d_model = x2d.shape
    bm = _tile(seq_len, 1024, 8)
    bn = _tile(d_model, 512, LANES)
    tpb = seq_len // bm

    def body(accs, e_refs, o_refs):
        o_refs[0][...] = e_refs[0][...] + e_refs[1][...] * accs[0]

    return _matmul([a], [(0, w, lambda j: j)],
                   [(x2d, (bm, bn), lambda i, j: (i, j)), (gate, (None, 1, bn), lambda i, j: (i // tpb, 0, j))],
                   [(jax.ShapeDtypeStruct((m, d_model), F32), (bm, bn), lambda i, j: (i, j))],
                   body, bm=bm, bn=bn, n_col_blocks=d_model // bn, bk=bk, name=name)[0]


def _ffn_up(h, w_gate, w_up):
    m, d_ff = h.shape[0], w_gate.shape[1]
    bm = _tile(m, 1024, 8)
    bn = _tile(d_ff, 512, LANES)

    def body(accs, e_refs, o_refs):
        o_refs[0][...] = (_silu(accs[0]) * accs[1]).astype(BF16)

    return _matmul([h], [(0, w_gate, lambda j: j), (0, w_up, lambda j: j)], [],
                   [(jax.ShapeDtypeStruct((m, d_ff), BF16), (bm, bn), lambda i, j: (i, j))],
                   body, bm=bm, bn=bn, n_col_blocks=d_ff // bn, name="ffn_up")[0]


def _cumsum_rows(x, reverse):
    n = x.shape[0]
    rows = lax.broadcasted_iota(jnp.int32, x.shape, 0)
    s = 1
    while s < n:
        if reverse:
            x = x + jnp.where(rows < n - s, pltpu.roll(x, n - s, 0), 0.0)
        else:
            x = x + jnp.where(rows >= s, pltpu.roll(x, s, 0), 0.0)
        s *= 2
    return x


_NT = (((1,), (1,)), ((), ()))
_TN = (((0,), (0,)), ((), ()))


def _scan_refs(b, reverse):
    n_sub = CHUNK // SUB
    bc = lambda r: jnp.broadcast_to(b[r:r + 1, :], (SUB, b.shape[1]))
    zero = jnp.zeros((SUB, b.shape[1]), F32)
    if reverse:
        start = [bc(SUB * (a + 1)) for a in range(n_sub - 1)] + [zero]
        stop = [bc(SUB * a) for a in range(n_sub)]
        mid = [bc(SUB * a + SUB // 2) for a in range(n_sub)]
        rho, b_end = b[HALF:HALF + 1, :], b[0:1, :]
    else:
        start = [zero] + [bc(SUB * a - 1) for a in range(1, n_sub)]
        stop = [bc(SUB * a + SUB - 1) for a in range(n_sub)]
        mid = [bc(SUB * a + SUB // 2 - 1) for a in range(n_sub)]
        rho, b_end = b[HALF - 1:HALF, :], b[CHUNK - 1:CHUNK, :]
    cat = lambda parts: jnp.concatenate(parts, axis=0)
    return cat(start), cat(stop), cat(mid), rho, b_end


def _scan_kernel(*refs, reverse, finalize, tb, hb):
    n_in = 8 if finalize else 5
    q_ref, v_ref, g_ref, k_ref, s0_ref = refs[:5]
    if finalize:
        ofw_ref, sog_ref, gon_ref = refs[5:8]
    o_ref, sfin_ref = refs[n_in:n_in + 2]
    st_ref, dec_ref = refs[n_in + 2:n_in + 4]
    q0_ref, k0_ref, q1_ref, k1_ref, q2_ref, k2_ref, qs_ref, ks_ref = refs[n_in + 4:]
    step = pl.program_id(2)
    n_chunks = tb // CHUNK

    @pl.when(step == 0)
    def _():
        st_ref[...] = s0_ref[...]

    safe = jnp.min(g_ref[...]) > -SAFE_STEP_LOG_DECAY

    tri_r = lax.broadcasted_iota(jnp.int32, (CHUNK, CHUNK), 0)
    tri_c = lax.broadcasted_iota(jnp.int32, (CHUNK, CHUNK), 1)
    causal = (tri_c >= tri_r) if reverse else (tri_c <= tri_r)
    pt = (CHUNK - 1 - tri_r) if reverse else tri_r
    ps = (CHUNK - 1 - tri_c) if reverse else tri_c
    sub_t, sub_s = pt // SUB, ps // SUB
    m0 = (sub_t == sub_s) & (ps <= pt)
    m1 = (sub_t == sub_s + 1) & (sub_t % 2 == 1)
    m2 = (pt >= HALF) & (ps < HALF)
    end_row = 0 if reverse else CHUNK - 1

    def emit(rows, lanes, hh, o):
        if finalize:
            tot = o + ofw_ref[rows, lanes]
            ms = jnp.mean(tot * tot, axis=-1, keepdims=True)
            y = tot * lax.rsqrt(ms + EPS) * gon_ref[:, lanes]
            o_ref[rows, lanes] = (y * sog_ref[rows, lanes].astype(F32)).astype(o_ref.dtype)
        else:
            o_ref[rows, lanes] = o

    def tile(c, hh):
        return slice(c * CHUNK, (c + 1) * CHUNK), slice(hh * HEAD_DIM, (hh + 1) * HEAD_DIM)

    def prepare(c, hh):
        rows, lanes = tile(c, hh)
        b = _cumsum_rows(g_ref[rows, lanes], reverse)
        q = q_ref[rows, lanes].astype(F32)
        k = k_ref[rows, lanes].astype(F32)
        start, stop, mid, rho, b_end = _scan_refs(b, reverse)
        q0_ref[rows, lanes] = (q * jnp.exp(b - mid)).astype(BF16)
        k0_ref[rows, lanes] = (k * jnp.exp(mid - b)).astype(BF16)
        q1_ref[rows, lanes] = (q * jnp.exp(b - start)).astype(BF16)
        k1_ref[rows, lanes] = (k * jnp.exp(stop - b)).astype(BF16)
        e2 = jnp.exp(-jnp.abs(b - rho))
        q2_ref[rows, lanes] = (q * e2).astype(BF16)
        k2_ref[rows, lanes] = (k * e2).astype(BF16)
        qs_ref[rows, lanes] = (q * jnp.exp(b)).astype(BF16)
        ks_ref[rows, lanes] = (k * jnp.exp(b_end - b)).astype(BF16)
        dec_ref[c, :, lanes] = jnp.exp(b_end)

    def instance(c, hh):
        rows, lanes = tile(c, hh)
        nt = lambda a, bb: lax.dot_general(a, bb, _NT, preferred_element_type=F32)
        s0 = nt(q0_ref[rows, lanes], k0_ref[rows, lanes])
        s1 = nt(q1_ref[rows, lanes], k1_ref[rows, lanes])
        s2 = nt(q2_ref[rows, lanes], k2_ref[rows, lanes])
        p = jnp.where(m0, s0, jnp.where(m1, s1, jnp.where(m2, s2, 0.0))).astype(BF16)
        v = v_ref[rows, lanes]
        st = st_ref[hh]
        o = nt(qs_ref[rows, lanes], st.astype(BF16)) + jnp.dot(p, v, preferred_element_type=F32)
        upd = lax.dot_general(v, ks_ref[rows, lanes], _TN, preferred_element_type=F32)
        st_ref[hh] = st * dec_ref[c, :, lanes] + upd
        emit(rows, lanes, hh, o)

    def exact_instance(c, hh):
        rows, lanes = tile(c, hh)
        b = _cumsum_rows(g_ref[rows, lanes], reverse)
        q = q_ref[rows, lanes].astype(F32)
        k = k_ref[rows, lanes].astype(F32)
        v = v_ref[rows, lanes]
        st = st_ref[hh]
        b_end = b[end_row:end_row + 1, :]
        row_id = lax.broadcasted_iota(jnp.int32, b.shape, 0)

        def col(s, sc):
            pick = row_id == s
            b_s = jnp.sum(jnp.where(pick, b, 0.0), axis=0, keepdims=True)
            k_s = jnp.sum(jnp.where(pick, k, 0.0), axis=0, keepdims=True)
            d = jnp.exp(jnp.minimum(b - b_s, 0.0))
            cs = jnp.sum(q * d * k_s, axis=-1, keepdims=True)
            return jnp.where(tri_c == s, cs, sc)

        scores = lax.fori_loop(0, CHUNK, col, jnp.zeros((CHUNK, CHUNK), F32))
        o_int = lax.dot_general((q * jnp.exp(b)).astype(BF16), st.astype(BF16), _NT,
                                preferred_element_type=F32)
        upd = lax.dot_general(v, (k * jnp.exp(b_end - b)).astype(BF16), _TN,
                              preferred_element_type=F32)
        p = jnp.where(causal, scores, 0.0).astype(BF16)
        o = o_int + jnp.dot(p, v, preferred_element_type=F32)
        st_ref[hh] = st * jnp.exp(b_end) + upd
        emit(rows, lanes, hh, o)

    def run(fast):
        order = range(n_chunks - 1, -1, -1) if reverse else range(n_chunks)
        if fast:
            for c in order:
                for hh in range(hb):
                    prepare(c, hh)
        for c in order:
            for hh in range(hb):
                (instance if fast else exact_instance)(c, hh)

    @pl.when(safe)
    def _():
        run(True)

    @pl.when(jnp.logical_not(safe))
    def _():
        run(False)

    @pl.when(step == pl.num_programs(2) - 1)
    def _():
        sfin_ref[...] = st_ref[...]


def _scan(z1, gf, kf, s0, nb, seq_len, d_a, *, reverse, fin=None):
    heads = d_a // HEAD_DIM
    hb = 4 if heads % 4 == 0 else 1
    width = hb * HEAD_DIM
    tb = _tile(seq_len, 256, CHUNK)
    nt = seq_len // tb
    ng = heads // hb
    gcols = d_a // width
    dirn = 1 if reverse else 0

    def row(b, i):
        return b * nt + ((nt - 1 - i) if reverse else i)

    in_specs = [pl.BlockSpec((tb, width), lambda b, h, i: (row(b, i), h)),
                pl.BlockSpec((tb, width), lambda b, h, i: (row(b, i), gcols + h)),
                pl.BlockSpec((tb, width), lambda b, h, i: (row(b, i), dirn * gcols + h)),
                pl.BlockSpec((tb, width), lambda b, h, i: (row(b, i), dirn * gcols + h)),
                pl.BlockSpec((None, hb, HEAD_DIM, HEAD_DIM), lambda b, h, i: (b, h, 0, 0))]
    args = [z1, z1, gf, kf, s0]
    if fin is not None:
        o_fw, g_onorm = fin
        in_specs += [pl.BlockSpec((tb, width), lambda b, h, i: (row(b, i), h)),
                     pl.BlockSpec((tb, width), lambda b, h, i: (row(b, i), 2 * gcols + h)),
                     pl.BlockSpec((1, width), lambda b, h, i: (0, h))]
        args += [o_fw, z1, g_onorm.reshape(1, d_a)]
    out_dtype = BF16 if fin is not None else F32
    out_specs = [pl.BlockSpec((tb, width), lambda b, h, i: (row(b, i), h)),
                 pl.BlockSpec((None, hb, HEAD_DIM, HEAD_DIM), lambda b, h, i: (b, h, 0, 0))]
    out_shape = [jax.ShapeDtypeStruct((nb * seq_len, d_a), out_dtype),
                 jax.ShapeDtypeStruct((nb, heads, HEAD_DIM, HEAD_DIM), F32)]
    return pl.pallas_call(
        functools.partial(_scan_kernel, reverse=reverse, finalize=fin is not None, tb=tb, hb=hb),
        grid=(nb, ng, nt),
        in_specs=in_specs, out_specs=out_specs, out_shape=out_shape,
        scratch_shapes=[pltpu.VMEM((hb, HEAD_DIM, HEAD_DIM), F32),
                        pltpu.VMEM((tb // CHUNK, 1, width), F32)]
                       + [pltpu.VMEM((tb, width), BF16) for _ in range(8)],
        compiler_params=_params(("arbitrary", "arbitrary", "arbitrary"), 32 * MIB),
        name="scan_bw" if reverse else "scan_fw",
    )(*args)


def _conv_kernel(cur_ref, prev_ref, next_ref, w_ref, cb_ref, lg_ref, lb_ref, o_ref, ext_ref, acc_ref,
                 *, tb, taps):
    i = pl.program_id(1)
    nt = pl.num_programs(1)
    pad = (taps - 1) // 2
    d_b = cur_ref.shape[1]
    n_lt = d_b // LANES
    zeros = jnp.zeros((CONV_HALO, LANES), F32)
    for lt in range(n_lt):
        lanes = slice(lt * LANES, (lt + 1) * LANES)
        ext_ref[lt, 0:CONV_HALO, :] = jnp.where(i > 0, prev_ref[:, lanes].astype(F32), zeros)
        ext_ref[lt, CONV_HALO:CONV_HALO + tb, :] = cur_ref[:, lanes].astype(F32)
        ext_ref[lt, CONV_HALO + tb:, :] = jnp.where(i < nt - 1, next_ref[:, lanes].astype(F32), zeros)
    rg = 64

    def lane_tile(lt, carry):
        for g in range(tb // rg):
            base = g * rg + CONV_HALO - pad
            acc = jnp.zeros((rg, LANES), F32)
            for j in range(taps):
                acc = acc + w_ref[lt, j:j + 1, :] * ext_ref[lt, base + j:base + j + rg, :]
            acc_ref[lt, g * rg:(g + 1) * rg, :] = acc
        return carry

    lax.fori_loop(0, n_lt, lane_tile, 0)
    tiles = [acc_ref[lt] + cb_ref[:, lt * LANES:(lt + 1) * LANES] for lt in range(n_lt)]
    mu = sum(jnp.sum(t, axis=-1, keepdims=True) for t in tiles) * (1.0 / d_b)
    cen = [t - mu for t in tiles]
    var = sum(jnp.sum(c * c, axis=-1, keepdims=True) for c in cen) * (1.0 / d_b)
    inv = lax.rsqrt(var + EPS)
    for lt in range(n_lt):
        lanes = slice(lt * LANES, (lt + 1) * LANES)
        hn = cen[lt] * inv * lg_ref[:, lanes] + lb_ref[:, lanes]
        o_ref[:, lanes] = _silu(hn).astype(o_ref.dtype)


def _conv_module(hglu, conv_w, conv_b, ln_g, ln_b, nb, seq_len):
    m, d_b = hglu.shape
    taps = conv_w.shape[0]
    assert (taps - 1) // 2 < CONV_HALO
    tb = _tile(seq_len, 256, 64)
    nt = seq_len // tb
    hpb = tb // CONV_HALO
    nh = seq_len // CONV_HALO

    def prev_map(b, i):
        return (b * nh + jnp.maximum(i * hpb - 1, 0), 0)

    def next_map(b, i):
        return (b * nh + jnp.minimum((i + 1) * hpb, nh - 1), 0)

    vec = lambda: pl.BlockSpec((1, d_b), lambda b, i: (0, 0))
    return pl.pallas_call(
        functools.partial(_conv_kernel, tb=tb, taps=taps),
        grid=(nb, nt),
        in_specs=[pl.BlockSpec((tb, d_b), lambda b, i: (b * nt + i, 0)),
                  pl.BlockSpec((CONV_HALO, d_b), prev_map),
                  pl.BlockSpec((CONV_HALO, d_b), next_map),
                  pl.BlockSpec((d_b // LANES, taps, LANES), lambda b, i: (0, 0, 0)),
                  vec(), vec(), vec()],
        out_specs=pl.BlockSpec((tb, d_b), lambda b, i: (b * nt + i, 0)),
        out_shape=jax.ShapeDtypeStruct((m, d_b), BF16),
        scratch_shapes=[pltpu.VMEM((d_b // LANES, tb + 2 * CONV_HALO, LANES), F32),
                        pltpu.VMEM((d_b // LANES, tb, LANES), F32)],
        compiler_params=_params(("arbitrary", "arbitrary"), 12 * tb * d_b * 4 + 8 * MIB),
        name="conv_module",
    )(hglu, hglu, hglu, conv_w.reshape(taps, d_b // LANES, LANES).transpose(1, 0, 2), conv_b.reshape(1, d_b),
      ln_g.reshape(1, d_b), ln_b.reshape(1, d_b))


def _pos_tables(seq_len, dim):
    rows = seq_len // GRID_W
    quarter = dim // 4
    omega = 1.0 / (POS_BASE ** (jnp.arange(quarter, dtype=F32) / quarter))
    ang_r = jnp.arange(rows, dtype=F32)[:, None] * omega
    ang_c = jnp.arange(GRID_W, dtype=F32)[:, None] * omega
    prow = jnp.concatenate([jnp.sin(ang_r), jnp.cos(ang_r)], axis=-1)
    pcol = jnp.concatenate([jnp.sin(ang_c), jnp.cos(ang_c)], axis=-1)
    return prow, pcol


def _mixer_inputs(h, w_in, lbvec, d_a, d_b, d_model):
    z1 = _inproj_main(h, w_in, d_a, d_b, d_model)
    gf, kf = _inproj_forget(h, w_in, lbvec, d_a)
    return z1, gf, kf


def kernel(x, c, ctx, c_ctx, w_mod, b_mod, g_norm1, w_in, lb_logits, g_onorm, w_a, conv_w, conv_b,
           ln_g, ln_b, w_b, w_o, g_norm2, w_ffn_gate, w_ffn_up, w_ffn_down, g_final):
    bsz, seq_len, d_model = x.shape
    ctx_len = ctx.shape[1]
    depth = w_mod.shape[0]
    d_a = lb_logits.shape[-1]
    d_b = conv_b.shape[-1]
    d_ff = w_ffn_gate.shape[-1]
    heads = d_a // HEAD_DIM
    assert bsz + 1 <= 8

    lb_all = jnp.cumsum(jax.nn.softmax(lb_logits.astype(F32), axis=0), axis=0)
    lb_all = lb_all - lb_all[0:1]
    lb_flat = lb_all.reshape(depth, 1, 2 * d_a)
    lbvecs = jnp.stack([jnp.log(lb_flat), jnp.log1p(-lb_flat), 1.0 - lb_flat], axis=1)

    cond = jnp.zeros((8, d_model), F32).at[:bsz].set(c).at[bsz].set(c_ctx)
    mod = _modvec(cond, w_mod, b_mod)

    ff_pad = (-d_ff) % 1024
    wb = lambda w: w.astype(BF16)
    w_in_b, w_a_b, w_b_b, w_o_b = wb(w_in), wb(w_a), wb(w_b), wb(w_o)
    w_gate_b = jnp.pad(wb(w_ffn_gate), ((0, 0), (0, 0), (0, ff_pad)))
    w_up_b = jnp.pad(wb(w_ffn_up), ((0, 0), (0, 0), (0, ff_pad)))
    w_down_b = jnp.pad(wb(w_ffn_down), ((0, 0), (0, ff_pad), (0, 0)))
    d_ffp = d_ff + ff_pad
    bk_down = _tile(d_ffp, 4096, LANES)

    pos = _pos_tables(seq_len, d_model)
    xl = x.reshape(bsz * seq_len, d_model)
    xc = ctx.reshape(bsz * ctx_len, d_model)
    s_zero = jnp.zeros((bsz, heads, HEAD_DIM, HEAD_DIM), F32)
    gate_col0 = 3 * d_a

    for layer in range(depth):
        last = layer == depth - 1
        mods = mod[layer].reshape(8, 6, d_model)
        lat = [mods[:bsz, k][:, None, :] for k in range(6)]
        cm = [jnp.broadcast_to(mods[bsz, k][None, None, :], (bsz, 1, d_model)) for k in range(6)]
        sh1, sc1, gt1, sh2, sc2, gt2 = lat
        sh1c, sc1c, gt1c, sh2c, sc2c, gt2c = cm

        if layer == 0:
            h, xl = _modulate(xl, g_norm1[layer], sh1, sc1, seq_len, pos=pos)
        else:
            h = _modulate(xl, g_norm1[layer], sh1, sc1, seq_len)
        hc = _modulate(xc, g_norm1[layer], sh1c, sc1c, ctx_len)

        z1c, gfc, kfc = _mixer_inputs(hc, w_in_b[layer], lbvecs[layer], d_a, d_b, d_model)
        oc_fw, s_fw = _scan(z1c, gfc, kfc, s_zero, bsz, ctx_len, d_a, reverse=False)
        ac, s_bw = _scan(z1c, gfc, kfc, s_zero, bsz, ctx_len, d_a, reverse=True,
                         fin=(oc_fw, g_onorm[layer]))

        z1, gf, kf = _mixer_inputs(h, w_in_b[layer], lbvecs[layer], d_a, d_b, d_model)
        o_fw, _ = _scan(z1, gf, kf, s_fw, bsz, seq_len, d_a, reverse=False)
        a_in, _ = _scan(z1, gf, kf, s_bw, bsz, seq_len, d_a, reverse=True, fin=(o_fw, g_onorm[layer]))

        hglu = _inproj_glu(h, w_in_b[layer], d_a, d_b)
        b_in = _conv_module(hglu, conv_w[layer], conv_b[layer], ln_g[layer], ln_b[layer], bsz, seq_len)
        mrg = _merge(a_in, b_in, w_a_b[layer], w_b_b[layer], z1, gate_col0)
        xl = _proj_residual(mrg, w_o_b[layer], xl, gt1, seq_len, name="out_proj")
        h2 = _modulate(xl, g_norm2[layer], sh2, sc2, seq_len)
        u = _ffn_up(h2, w_gate_b[layer], w_up_b[layer])
        xl = _proj_residual(u, w_down_b[layer], xl, gt2, seq_len, bk=bk_down, name="ffn_down")

        if not last:
            hgluc = _inproj_glu(hc, w_in_b[layer], d_a, d_b)
            bc = _conv_module(hgluc, conv_w[layer], conv_b[layer], ln_g[layer], ln_b[layer], bsz, ctx_len)
            mrgc = _merge(ac, bc, w_a_b[layer], w_b_b[layer], z1c, gate_col0)
            xc = _proj_residual(mrgc, w_o_b[layer], xc, gt1c, ctx_len, name="out_proj_ctx")
            h2c = _modulate(xc, g_norm2[layer], sh2c, sc2c, ctx_len)
            uc = _ffn_up(h2c, w_gate_b[layer], w_up_b[layer])
            xc = _proj_residual(uc, w_down_b[layer], xc, gt2c, ctx_len, bk=bk_down, name="ffn_down_ctx")

    return _final_norm(xl, g_final).reshape(bsz, seq_len, d_model)
```

```python
import functools

import jax
import jax.numpy as jnp
from jax import lax
from jax.experimental import pallas as pl
from jax.experimental.pallas import tpu as pltpu

EPS = 1e-6
GRID_W = 64
POS_BASE = 10000.0
HEAD_DIM = 128
CHUNK = 64
HALF = CHUNK // 2
SUB = 16
SAFE_STEP_LOG_DECAY = 75.0 / (SUB // 2)
CONV_HALO = 16
LANES = 128
MIB = 1 << 20
VMEM_CAP = 60 * MIB

F32 = jnp.float32
BF16 = jnp.bfloat16


def _tile(n, pref, mult):
    t = (min(pref, n) // mult) * mult
    while t >= mult:
        if n % t == 0:
            return t
        t -= mult
    raise ValueError(f"no tile for {n} (pref {pref}, mult {mult})")


def _params(sem, vmem_bytes):
    return pltpu.CompilerParams(dimension_semantics=sem,
                                vmem_limit_bytes=int(min(max(vmem_bytes, 16 * MIB), VMEM_CAP)))


def _sigmoid(x):
    return jax.nn.sigmoid(x)


def _silu(x):
    return x * jax.nn.sigmoid(x)


def _log_sigmoid(x):
    return jnp.minimum(x, 0.0) - jnp.log1p(jnp.exp(-jnp.abs(x)))


def _modvec_kernel(a_ref, w_ref, b_ref, o_ref):
    a = a_ref[...]
    act = (a * _sigmoid(a)).astype(BF16)
    o_ref[...] = jnp.dot(act, w_ref[...].astype(BF16), preferred_element_type=F32) + b_ref[...]


def _modvec(cond_rows, w_mod, b_mod):
    depth, d, md = w_mod.shape
    bn = _tile(md, 512, LANES)
    return pl.pallas_call(
        _modvec_kernel,
        grid=(depth, md // bn),
        in_specs=[pl.BlockSpec((8, d), lambda l, j: (0, 0)),
                  pl.BlockSpec((None, d, bn), lambda l, j: (l, 0, j)),
                  pl.BlockSpec((None, 1, bn), lambda l, j: (l, 0, j))],
        out_specs=pl.BlockSpec((None, 8, bn), lambda l, j: (l, 0, j)),
        out_shape=jax.ShapeDtypeStruct((depth, 8, md), F32),
        compiler_params=_params(("arbitrary", "arbitrary"), 2 * d * bn * 4 + d * bn * 2 + 4 * MIB),
        name="modvec",
    )(cond_rows, w_mod, b_mod.reshape(depth, 1, md))


def _modulate_kernel(*refs, with_pos, bm):
    if with_pos:
        x_ref, g_ref, sh_ref, sc_ref, prow_ref, pcol_ref, h_ref, xo_ref = refs
        half = x_ref.shape[1] // 2
        for r in range(bm // GRID_W):
            rows = slice(r * GRID_W, (r + 1) * GRID_W)
            xo_ref[rows, :half] = x_ref[rows, :half] + prow_ref[r:r + 1, :]
            xo_ref[rows, half:] = x_ref[rows, half:] + pcol_ref[...]
        x = xo_ref[...]
    else:
        x_ref, g_ref, sh_ref, sc_ref, h_ref = refs
        x = x_ref[...]
    ms = jnp.mean(x * x, axis=-1, keepdims=True)
    y = x * lax.rsqrt(ms + EPS) * g_ref[...]
    h_ref[...] = (y * (1.0 + sc_ref[...]) + sh_ref[...]).astype(h_ref.dtype)


def _modulate(x2d, gain, shift, scale, seq_len, pos=None):
    m, d = x2d.shape
    with_pos = pos is not None
    bm = _tile(seq_len, 256, GRID_W if with_pos else 8)
    tpb = seq_len // bm
    in_specs = [pl.BlockSpec((bm, d), lambda i: (i, 0)),
                pl.BlockSpec((1, d), lambda i: (0, 0)),
                pl.BlockSpec((None, 1, d), lambda i: (i // tpb, 0, 0)),
                pl.BlockSpec((None, 1, d), lambda i: (i // tpb, 0, 0))]
    args = [x2d, gain.reshape(1, d), shift, scale]
    out_specs = [pl.BlockSpec((bm, d), lambda i: (i, 0))]
    out_shape = [jax.ShapeDtypeStruct((m, d), BF16)]
    if with_pos:
        prow, pcol = pos
        rpt = bm // GRID_W
        in_specs += [pl.BlockSpec((None, rpt, d // 2), lambda i: (i % tpb, 0, 0)),
                     pl.BlockSpec((GRID_W, d // 2), lambda i: (0, 0))]
        args += [prow.reshape(tpb, rpt, d // 2), pcol]
        out_specs.append(pl.BlockSpec((bm, d), lambda i: (i, 0)))
        out_shape.append(jax.ShapeDtypeStruct((m, d), F32))
    out = pl.pallas_call(
        functools.partial(_modulate_kernel, with_pos=with_pos, bm=bm),
        grid=(m // bm,),
        in_specs=in_specs, out_specs=out_specs, out_shape=out_shape,
        compiler_params=_params(("arbitrary",), 2 * bm * d * (4 + 2 + (4 if with_pos else 0)) + 8 * MIB),
        name="modulate_pos" if with_pos else "modulate",
    )(*args)
    return (out[0], out[1]) if with_pos else out[0]


def _final_norm_kernel(x_ref, g_ref, o_ref):
    x = x_ref[...]
    ms = jnp.mean(x * x, axis=-1, keepdims=True)
    o_ref[...] = x * lax.rsqrt(ms + EPS) * g_ref[...]


def _final_norm(x2d, gain):
    m, d = x2d.shape
    bm = _tile(m, 256, 8)
    return pl.pallas_call(
        _final_norm_kernel,
        grid=(m // bm,),
        in_specs=[pl.BlockSpec((bm, d), lambda i: (i, 0)), pl.BlockSpec((1, d), lambda i: (0, 0))],
        out_specs=pl.BlockSpec((bm, d), lambda i: (i, 0)),
        out_shape=jax.ShapeDtypeStruct((m, d), F32),
        compiler_params=_params(("arbitrary",), 4 * bm * d * 4 + 8 * MIB),
        name="final_norm",
    )(x2d, gain.reshape(1, d))


def _matmul_kernel(*refs, n_a, dots, n_extra, nk, body):
    a_refs = refs[:n_a]
    w_refs = refs[n_a:n_a + len(dots)]
    e_refs = refs[n_a + len(dots):n_a + len(dots) + n_extra]
    rest = refs[n_a + len(dots) + n_extra:]
    if nk == 1:
        o_refs = rest
        accs = [jnp.dot(a_refs[ai][...], w_refs[d][...], preferred_element_type=F32)
                for d, ai in enumerate(dots)]
        body(accs, e_refs, o_refs)
        return
    o_refs, acc_refs = rest[:len(rest) - len(dots)], rest[len(rest) - len(dots):]
    k = pl.program_id(2)
    for d, ai in enumerate(dots):
        part = jnp.dot(a_refs[ai][...], w_refs[d][...], preferred_element_type=F32)

        @pl.when(k == 0)
        def _(part=part, d=d):
            acc_refs[d][...] = part

        @pl.when(k > 0)
        def _(part=part, d=d):
            acc_refs[d][...] += part

    @pl.when(k == nk - 1)
    def _():
        body([r[...] for r in acc_refs], e_refs, o_refs)


def _matmul(a_list, dots, extras, outs, body, *, layer, bm, bn, n_col_blocks, bk=None, name):
    m, kdim = a_list[0].shape
    bk = kdim if bk is None else bk
    nk = kdim // bk
    assert m % bm == 0 and kdim % bk == 0
    if nk == 1:
        wrap = lambda f: (lambda i, j: f(i, j))
        a_map = lambda i, j: (i, 0)
        w_map = lambda cf: (lambda i, j: (layer, 0, cf(j)))
        grid = (m // bm, n_col_blocks)
        sem = ("arbitrary", "arbitrary")
    else:
        wrap = lambda f: (lambda i, j, k: f(i, j))
        a_map = lambda i, j, k: (i, k)
        w_map = lambda cf: (lambda i, j, k: (layer, k, cf(j)))
        grid = (m // bm, n_col_blocks, nk)
        sem = ("arbitrary", "arbitrary", "arbitrary")
    a_k = [bk if nk > 1 else a.shape[1] for a in a_list]
    assert all(w.shape[1] == a_list[ai].shape[1] for (ai, w, _) in dots)
    in_specs = [pl.BlockSpec((bm, ak), a_map) for ak in a_k]
    in_specs += [pl.BlockSpec((None, a_k[ai], bn), w_map(cf)) for (ai, _, cf) in dots]
    in_specs += [pl.BlockSpec(bs, wrap(f)) for (_, bs, f) in extras]
    out_specs = [pl.BlockSpec(bs, wrap(f)) for (_, bs, f) in outs]
    scratch = [pltpu.VMEM((bm, bn), F32) for _ in dots] if nk > 1 else []

    def nbytes(shape, dtype):
        n = 1
        for s in shape:
            n *= (s or 1)
        return n * jnp.dtype(dtype).itemsize

    vmem = 2 * (sum(bm * ak * 2 for ak in a_k) + sum(a_k[ai] * bn * 2 for (ai, _, _) in dots))
    vmem += 2 * sum(nbytes(bs, arr.dtype) for (arr, bs, _) in extras)
    vmem += 2 * sum(nbytes(bs, sd.dtype) for (sd, bs, _) in outs)
    vmem += (2 * len(dots) + 1) * bm * bn * 4 + 4 * MIB
    return pl.pallas_call(
        functools.partial(_matmul_kernel, n_a=len(a_list), dots=[ai for (ai, _, _) in dots],
                          n_extra=len(extras), nk=nk, body=body),
        grid=grid, in_specs=in_specs, out_specs=out_specs,
        out_shape=[sd for (sd, _, _) in outs],
        scratch_shapes=scratch,
        compiler_params=_params(sem, vmem),
        name=name,
    )(*a_list, *[w for (_, w, _) in dots], *[arr for (arr, _, _) in extras])


def _inproj_main(h, w_in, layer, d_a, d_b, d_model):
    m = h.shape[0]
    bm = _tile(m, 1024, 8)
    bn = _tile(d_a, 1024, LANES)
    groups = [(0, d_a, "silu"), (d_a, d_a, "id"), (4 * d_a, d_a, "silu"),
              (5 * d_a + 2 * d_b, 2 * d_model, "sigmoid")]
    ranges, j0 = [], 0
    for (src, width, act) in groups:
        assert src % bn == 0 and width % bn == 0
        ranges.append((j0, j0 + width // bn, src // bn - j0, act))
        j0 += width // bn
    n_blocks = j0

    def col_fn(j):
        blk = j + ranges[0][2]
        for (a0, _, off, _) in ranges[1:]:
            blk = jnp.where(j >= a0, j + off, blk)
        return blk

    def body(accs, e_refs, o_refs):
        j = pl.program_id(1)
        x = accs[0]
        s = _sigmoid(x)
        out = x * s
        for (a0, a1, _, act) in ranges:
            if act != "silu":
                out = jnp.where((j >= a0) & (j < a1), x if act == "id" else s, out)
        o_refs[0][...] = out.astype(BF16)

    n_out = n_blocks * bn
    return _matmul([h], [(0, w_in, col_fn)], [],
                   [(jax.ShapeDtypeStruct((m, n_out), BF16), (bm, bn), lambda i, j: (i, j))],
                   body, layer=layer, bm=bm, bn=bn, n_col_blocks=n_blocks, name="inproj_main")[0]


def _inproj_forget(h, w_in, layer, lbvec, d_a):
    m = h.shape[0]
    bm = _tile(m, 1024, 8)
    bn = _tile(d_a, 512, LANES)
    off = (2 * d_a) // bn

    def body(accs, e_refs, o_refs):
        x = accs[0]
        log_lb, log_1m_lb = e_refs[0][0], e_refs[0][1]
        a = jnp.broadcast_to(log_lb, x.shape)
        b = log_1m_lb + (jnp.minimum(x, 0.0) - jnp.log(1.0 + jnp.exp(-jnp.abs(x))))
        hi = jnp.maximum(a, b)
        log_f = hi + jnp.log(1.0 + jnp.exp(-jnp.abs(a - b)))
        o_refs[0][...] = log_f
        o_refs[1][...] = (1.0 - jnp.exp(log_f)).astype(BF16)

    return _matmul([h], [(0, w_in, lambda j: j + off)],
                   [(lbvec, (2, 1, bn), lambda i, j: (0, 0, j))],
                   [(jax.ShapeDtypeStruct((m, 2 * d_a), F32), (bm, bn), lambda i, j: (i, j)),
                    (jax.ShapeDtypeStruct((m, 2 * d_a), BF16), (bm, bn), lambda i, j: (i, j))],
                   body, layer=layer, bm=bm, bn=bn, n_col_blocks=(2 * d_a) // bn, name="inproj_forget")


def _inproj_glu(h, w_in, layer, d_a, d_b):
    m = h.shape[0]
    bm = _tile(m, 1024, 8)
    bn = _tile(d_b, 512, LANES)
    off_a, off_g = (5 * d_a) // bn, (5 * d_a + d_b) // bn

    def body(accs, e_refs, o_refs):
        o_refs[0][...] = (accs[0] * _sigmoid(accs[1])).astype(BF16)

    return _matmul([h], [(0, w_in, lambda j: j + off_a), (0, w_in, lambda j: j + off_g)], [],
                   [(jax.ShapeDtypeStruct((m, d_b), BF16), (bm, bn), lambda i, j: (i, j))],
                   body, layer=layer, bm=bm, bn=bn, n_col_blocks=d_b // bn, name="inproj_glu")[0]


def _merge(a_in, b_in, w_a, w_b, layer, z1, gate_col0):
    m, d_model = a_in.shape[0], w_a.shape[2]
    bm = _tile(m, 1024, 8)
    bn = _tile(d_model, 512, LANES)
    ga0 = gate_col0 // bn
    gb0 = (gate_col0 + d_model) // bn

    def body(accs, e_refs, o_refs):
        ga = e_refs[0][...].astype(F32)
        gb = e_refs[1][...].astype(F32)
        o_refs[0][...] = (ga * accs[0] + gb * accs[1]).astype(BF16)

    return _matmul([a_in, b_in], [(0, w_a, lambda j: j), (1, w_b, lambda j: j)],
                   [(z1, (bm, bn), lambda i, j: (i, j + ga0)), (z1, (bm, bn), lambda i, j: (i, j + gb0))],
                   [(jax.ShapeDtypeStruct((m, d_model), BF16), (bm, bn), lambda i, j: (i, j))],
                   body, layer=layer, bm=bm, bn=bn, n_col_blocks=d_model // bn, name="merge")[0]


def _proj_residual(a, w, layer, x2d, gate, seq_len, *, bm_pref=1024, name):
    m, d_model = x2d.shape
    bm = _tile(seq_len, bm_pref, 8)
    bn = _tile(d_model, 512, LANES)
    tpb = seq_len // bm

    def body(accs, e_refs, o_refs):
        o_refs[0][...] = e_refs[0][...] + e_refs[1][...] * accs[0]

    return _matmul([a], [(0, w, lambda j: j)],
                   [(x2d, (bm, bn), lambda i, j: (i, j)), (gate, (None, 1, bn), lambda i, j: (i // tpb, 0, j))],
                   [(jax.ShapeDtypeStruct((m, d_model), F32), (bm, bn), lambda i, j: (i, j))],
                   body, layer=layer, bm=bm, bn=bn, n_col_blocks=d_model // bn, name=name)[0]


def _ffn_up(h, w_gate, w_up, layer):
    m, d_ff = h.shape[0], w_gate.shape[2]
    bm = _tile(m, 1024, 8)
    bn = _tile(d_ff, 512, LANES)

    def body(accs, e_refs, o_refs):
        o_refs[0][...] = (_silu(accs[0]) * accs[1]).astype(BF16)

    return _matmul([h], [(0, w_gate, lambda j: j), (0, w_up, lambda j: j)], [],
                   [(jax.ShapeDtypeStruct((m, d_ff), BF16), (bm, bn), lambda i, j: (i, j))],
                   body, layer=layer, bm=bm, bn=bn, n_col_blocks=d_ff // bn, name="ffn_up")[0]


def _cumsum_rows(x, reverse):
    n, grp = x.shape[0], 8
    pos = lax.broadcasted_iota(jnp.int32, x.shape, 0) % grp
    s = 1
    while s < grp:
        if reverse:
            x = x + jnp.where(pos < grp - s, pltpu.roll(x, n - s, 0), 0.0)
        else:
            x = x + jnp.where(pos >= s, pltpu.roll(x, s, 0), 0.0)
        s *= 2
    groups = [x[i * grp:(i + 1) * grp] for i in range(n // grp)]
    total_row = 0 if reverse else grp - 1
    order = range(len(groups) - 2, -1, -1) if reverse else range(1, len(groups))
    off = None
    for i in order:
        prev = groups[i + 1] if reverse else groups[i - 1]
        off = jnp.broadcast_to(prev[total_row:total_row + 1, :], (grp, x.shape[1]))
        groups[i] = groups[i] + off
    return jnp.concatenate(groups, axis=0)


_NT = (((1,), (1,)), ((), ()))
_TN = (((0,), (0,)), ((), ()))


def _scan_refs(b, reverse):
    n_sub = CHUNK // SUB
    bc = lambda r: jnp.broadcast_to(b[r:r + 1, :], (SUB, b.shape[1]))
    zero = jnp.zeros((SUB, b.shape[1]), F32)
    if reverse:
        start = [bc(SUB * (a + 1)) for a in range(n_sub - 1)] + [zero]
        stop = [bc(SUB * a) for a in range(n_sub)]
        mid = [bc(SUB * a + SUB // 2) for a in range(n_sub)]
        rho, b_end = b[HALF:HALF + 1, :], b[0:1, :]
    else:
        start = [zero] + [bc(SUB * a - 1) for a in range(1, n_sub)]
        stop = [bc(SUB * a + SUB - 1) for a in range(n_sub)]
        mid = [bc(SUB * a + SUB // 2 - 1) for a in range(n_sub)]
        rho, b_end = b[HALF - 1:HALF, :], b[CHUNK - 1:CHUNK, :]
    cat = lambda parts: jnp.concatenate(parts, axis=0)
    return cat(start), cat(stop), cat(mid), rho, b_end


def _scan_kernel(*refs, reverse, finalize, tb, hb):
    n_in = 8 if finalize else 5
    q_ref, v_ref, g_ref, k_ref, s0_ref = refs[:5]
    if finalize:
        ofw_ref, sog_ref, gon_ref = refs[5:8]
    o_ref, sfin_ref = refs[n_in:n_in + 2]
    st_ref, dec_ref = refs[n_in + 2:n_in + 4]
    q0_ref, k0_ref, q1_ref, k1_ref, q2_ref, k2_ref, qs_ref, ks_ref, p_ref = refs[n_in + 4:]
    step = pl.program_id(2)
    n_chunks = tb // CHUNK

    @pl.when(step == 0)
    def _():
        st_ref[...] = s0_ref[...]

    g_min = g_ref[...]
    while g_min.shape[0] > 8:
        half = g_min.shape[0] // 2
        g_min = jnp.minimum(g_min[:half], g_min[half:])
    safe = jnp.min(g_min) > -SAFE_STEP_LOG_DECAY

    tri_r = lax.broadcasted_iota(jnp.int32, (CHUNK, CHUNK), 0)
    tri_c = lax.broadcasted_iota(jnp.int32, (CHUNK, CHUNK), 1)
    causal = (tri_c >= tri_r) if reverse else (tri_c <= tri_r)
    pt = (CHUNK - 1 - tri_r) if reverse else tri_r
    ps = (CHUNK - 1 - tri_c) if reverse else tri_c
    sub_t, sub_s = pt // SUB, ps // SUB
    m0 = (sub_t == sub_s) & (ps <= pt)
    m1 = (sub_t == sub_s + 1) & (sub_t % 2 == 1)
    m2 = (pt >= HALF) & (ps < HALF)
    end_row = 0 if reverse else CHUNK - 1

    def emit(rows, lanes, hh, o):
        if finalize:
            tot = o + ofw_ref[rows, lanes]
            ms = jnp.mean(tot * tot, axis=-1, keepdims=True)
            y = tot * lax.rsqrt(ms + EPS) * gon_ref[:, lanes]
            o_ref[rows, lanes] = (y * sog_ref[rows, lanes].astype(F32)).astype(o_ref.dtype)
        else:
            o_ref[rows, lanes] = o

    def tile(c, hh):
        return slice(c * CHUNK, (c + 1) * CHUNK), slice(hh * HEAD_DIM, (hh + 1) * HEAD_DIM)

    def prepare(c, hh):
        rows, lanes = tile(c, hh)
        b = _cumsum_rows(g_ref[rows, lanes], reverse)
        q = q_ref[rows, lanes].astype(F32)
        k = k_ref[rows, lanes].astype(F32)
        start, stop, mid, rho, b_end = _scan_refs(b, reverse)
        q0_ref[rows, lanes] = (q * jnp.exp(b - mid)).astype(BF16)
        k0_ref[rows, lanes] = (k * jnp.exp(mid - b)).astype(BF16)
        q1_ref[rows, lanes] = (q * jnp.exp(b - start)).astype(BF16)
        k1_ref[rows, lanes] = (k * jnp.exp(stop - b)).astype(BF16)
        e2 = jnp.exp(-jnp.abs(b - rho))
        q2_ref[rows, lanes] = (q * e2).astype(BF16)
        k2_ref[rows, lanes] = (k * e2).astype(BF16)
        qs_ref[rows, lanes] = (q * jnp.exp(b)).astype(BF16)
        ks_ref[rows, lanes] = (k * jnp.exp(b_end - b)).astype(BF16)
        dec_ref[c, :, lanes] = jnp.exp(b_end)

    nt = lambda a, bb: lax.dot_general(a, bb, _NT, preferred_element_type=F32)

    def scores(c, hh):
        rows, lanes = tile(c, hh)
        s0 = nt(q0_ref[rows, lanes], k0_ref[rows, lanes])
        s1 = nt(q1_ref[rows, lanes], k1_ref[rows, lanes])
        s2 = nt(q2_ref[rows, lanes], k2_ref[rows, lanes])
        p_ref[hh, rows, :] = jnp.where(m0, s0, jnp.where(m1, s1, jnp.where(m2, s2, 0.0))).astype(BF16)

    def instance(c, hh):
        rows, lanes = tile(c, hh)
        v = v_ref[rows, lanes]
        st = st_ref[hh]
        o = nt(qs_ref[rows, lanes], st.astype(BF16)) + jnp.dot(p_ref[hh, rows, :], v,
                                                                preferred_element_type=F32)
        upd = lax.dot_general(v, ks_ref[rows, lanes], _TN, preferred_element_type=F32)
        st_ref[hh] = st * dec_ref[c, :, lanes] + upd
        emit(rows, lanes, hh, o)

    def exact_instance(c, hh):
        rows, lanes = tile(c, hh)
        b = _cumsum_rows(g_ref[rows, lanes], reverse)
        q = q_ref[rows, lanes].astype(F32)
        k = k_ref[rows, lanes].astype(F32)
        v = v_ref[rows, lanes]
        st = st_ref[hh]
        b_end = b[end_row:end_row + 1, :]
        row_id = lax.broadcasted_iota(jnp.int32, b.shape, 0)

        def col(s, sc):
            pick = row_id == s
            b_s = jnp.sum(jnp.where(pick, b, 0.0), axis=0, keepdims=True)
            k_s = jnp.sum(jnp.where(pick, k, 0.0), axis=0, keepdims=True)
            d = jnp.exp(jnp.minimum(b - b_s, 0.0))
            cs = jnp.sum(q * d * k_s, axis=-1, keepdims=True)
            return jnp.where(tri_c == s, cs, sc)

        scores = lax.fori_loop(0, CHUNK, col, jnp.zeros((CHUNK, CHUNK), F32))
        o_int = lax.dot_general((q * jnp.exp(b)).astype(BF16), st.astype(BF16), _NT,
                                preferred_element_type=F32)
        upd = lax.dot_general(v, (k * jnp.exp(b_end - b)).astype(BF16), _TN,
                              preferred_element_type=F32)
        p = jnp.where(causal, scores, 0.0).astype(BF16)
        o = o_int + jnp.dot(p, v, preferred_element_type=F32)
        st_ref[hh] = st * jnp.exp(b_end) + upd
        emit(rows, lanes, hh, o)

    def run(fast):
        order = range(n_chunks - 1, -1, -1) if reverse else range(n_chunks)
        if fast:
            for stage in (prepare, scores):
                for c in order:
                    for hh in range(hb):
                        stage(c, hh)
        for c in order:
            for hh in range(hb):
                (instance if fast else exact_instance)(c, hh)

    @pl.when(safe)
    def _():
        run(True)

    @pl.when(jnp.logical_not(safe))
    def _():
        run(False)

    @pl.when(step == pl.num_programs(2) - 1)
    def _():
        sfin_ref[...] = st_ref[...]


def _scan(z1, gf, kf, s0, nb, seq_len, d_a, *, reverse, fin=None):
    heads = d_a // HEAD_DIM
    hb = 4 if heads % 4 == 0 else 1
    width = hb * HEAD_DIM
    tb = _tile(seq_len, 256, CHUNK)
    nt = seq_len // tb
    ng = heads // hb
    gcols = d_a // width
    dirn = 1 if reverse else 0

    def row(b, i):
        return b * nt + ((nt - 1 - i) if reverse else i)

    in_specs = [pl.BlockSpec((tb, width), lambda b, h, i: (row(b, i), h)),
                pl.BlockSpec((tb, width), lambda b, h, i: (row(b, i), gcols + h)),
                pl.BlockSpec((tb, width), lambda b, h, i: (row(b, i), dirn * gcols + h)),
                pl.BlockSpec((tb, width), lambda b, h, i: (row(b, i), dirn * gcols + h)),
                pl.BlockSpec((None, hb, HEAD_DIM, HEAD_DIM), lambda b, h, i: (b, h, 0, 0))]
    args = [z1, z1, gf, kf, s0]
    if fin is not None:
        o_fw, g_onorm = fin
        in_specs += [pl.BlockSpec((tb, width), lambda b, h, i: (row(b, i), h)),
                     pl.BlockSpec((tb, width), lambda b, h, i: (row(b, i), 2 * gcols + h)),
                     pl.BlockSpec((1, width), lambda b, h, i: (0, h))]
        args += [o_fw, z1, g_onorm.reshape(1, d_a)]
    out_dtype = BF16 if fin is not None else F32
    out_specs = [pl.BlockSpec((tb, width), lambda b, h, i: (row(b, i), h)),
                 pl.BlockSpec((None, hb, HEAD_DIM, HEAD_DIM), lambda b, h, i: (b, h, 0, 0))]
    out_shape = [jax.ShapeDtypeStruct((nb * seq_len, d_a), out_dtype),
                 jax.ShapeDtypeStruct((nb, heads, HEAD_DIM, HEAD_DIM), F32)]
    return pl.pallas_call(
        functools.partial(_scan_kernel, reverse=reverse, finalize=fin is not None, tb=tb, hb=hb),
        grid=(nb, ng, nt),
        in_specs=in_specs, out_specs=out_specs, out_shape=out_shape,
        scratch_shapes=[pltpu.VMEM((hb, HEAD_DIM, HEAD_DIM), F32),
                        pltpu.VMEM((tb // CHUNK, 1, width), F32)]
                       + [pltpu.VMEM((tb, width), BF16) for _ in range(8)]
                       + [pltpu.VMEM((hb, tb, CHUNK), BF16)],
        compiler_params=_params(("arbitrary", "arbitrary", "arbitrary"), 32 * MIB),
        name="scan_bw" if reverse else "scan_fw",
    )(*args)


def _conv_kernel(cur_ref, prev_ref, next_ref, w_ref, cb_ref, lg_ref, lb_ref, o_ref, ext_ref, acc_ref,
                 *, tb, taps):
    i = pl.program_id(1)
    nt = pl.num_programs(1)
    pad = (taps - 1) // 2
    d_b = cur_ref.shape[1]
    n_lt = d_b // LANES
    zeros = jnp.zeros((CONV_HALO, LANES), F32)
    for lt in range(n_lt):
        lanes = slice(lt * LANES, (lt + 1) * LANES)
        ext_ref[lt, 0:CONV_HALO, :] = jnp.where(i > 0, prev_ref[:, lanes].astype(F32), zeros)
        ext_ref[lt, CONV_HALO:CONV_HALO + tb, :] = cur_ref[:, lanes].astype(F32)
        ext_ref[lt, CONV_HALO + tb:, :] = jnp.where(i < nt - 1, next_ref[:, lanes].astype(F32), zeros)
    rg = 64

    def lane_tile(lt, carry):
        for g in range(tb // rg):
            base = g * rg + CONV_HALO - pad
            acc = jnp.zeros((rg, LANES), F32)
            for j in range(taps):
                acc = acc + w_ref[lt, j:j + 1, :] * ext_ref[lt, base + j:base + j + rg, :]
            acc_ref[lt, g * rg:(g + 1) * rg, :] = acc
        return carry

    lax.fori_loop(0, n_lt, lane_tile, 0)
    tiles = [acc_ref[lt] + cb_ref[:, lt * LANES:(lt + 1) * LANES] for lt in range(n_lt)]
    mu = sum(jnp.sum(t, axis=-1, keepdims=True) for t in tiles) * (1.0 / d_b)
    cen = [t - mu for t in tiles]
    var = sum(jnp.sum(c * c, axis=-1, keepdims=True) for c in cen) * (1.0 / d_b)
    inv = lax.rsqrt(var + EPS)
    for lt in range(n_lt):
        lanes = slice(lt * LANES, (lt + 1) * LANES)
        hn = cen[lt] * inv * lg_ref[:, lanes] + lb_ref[:, lanes]
        o_ref[:, lanes] = _silu(hn).astype(o_ref.dtype)


def _conv_module(hglu, conv_w, conv_b, ln_g, ln_b, nb, seq_len):
    m, d_b = hglu.shape
    taps = conv_w.shape[0]
    assert (taps - 1) // 2 < CONV_HALO
    tb = _tile(seq_len, 256, 64)
    nt = seq_len // tb
    hpb = tb // CONV_HALO
    nh = seq_len // CONV_HALO

    def prev_map(b, i):
        return (b * nh + jnp.maximum(i * hpb - 1, 0), 0)

    def next_map(b, i):
        return (b * nh + jnp.minimum((i + 1) * hpb, nh - 1), 0)

    vec = lambda: pl.BlockSpec((1, d_b), lambda b, i: (0, 0))
    return pl.pallas_call(
        functools.partial(_conv_kernel, tb=tb, taps=taps),
        grid=(nb, nt),
        in_specs=[pl.BlockSpec((tb, d_b), lambda b, i: (b * nt + i, 0)),
                  pl.BlockSpec((CONV_HALO, d_b), prev_map),
                  pl.BlockSpec((CONV_HALO, d_b), next_map),
                  pl.BlockSpec((d_b // LANES, taps, LANES), lambda b, i: (0, 0, 0)),
                  vec(), vec(), vec()],
        out_specs=pl.BlockSpec((tb, d_b), lambda b, i: (b * nt + i, 0)),
        out_shape=jax.ShapeDtypeStruct((m, d_b), BF16),
        scratch_shapes=[pltpu.VMEM((d_b // LANES, tb + 2 * CONV_HALO, LANES), F32),
                        pltpu.VMEM((d_b // LANES, tb, LANES), F32)],
        compiler_params=_params(("arbitrary", "arbitrary"), 12 * tb * d_b * 4 + 8 * MIB),
        name="conv_module",
    )(hglu, hglu, hglu, conv_w.reshape(taps, d_b // LANES, LANES).transpose(1, 0, 2), conv_b.reshape(1, d_b),
      ln_g.reshape(1, d_b), ln_b.reshape(1, d_b))


def _pos_tables(seq_len, dim):
    rows = seq_len // GRID_W
    quarter = dim // 4
    omega = 1.0 / (POS_BASE ** (jnp.arange(quarter, dtype=F32) / quarter))
    ang_r = jnp.arange(rows, dtype=F32)[:, None] * omega
    ang_c = jnp.arange(GRID_W, dtype=F32)[:, None] * omega
    prow = jnp.concatenate([jnp.sin(ang_r), jnp.cos(ang_r)], axis=-1)
    pcol = jnp.concatenate([jnp.sin(ang_c), jnp.cos(ang_c)], axis=-1)
    return prow, pcol


def _mixer_inputs(h, w_in, layer, lbvec, d_a, d_b, d_model):
    z1 = _inproj_main(h, w_in, layer, d_a, d_b, d_model)
    gf, kf = _inproj_forget(h, w_in, layer, lbvec, d_a)
    return z1, gf, kf


def kernel(x, c, ctx, c_ctx, w_mod, b_mod, g_norm1, w_in, lb_logits, g_onorm, w_a, conv_w, conv_b,
           ln_g, ln_b, w_b, w_o, g_norm2, w_ffn_gate, w_ffn_up, w_ffn_down, g_final):
    bsz, seq_len, d_model = x.shape
    ctx_len = ctx.shape[1]
    depth = w_mod.shape[0]
    d_a = lb_logits.shape[-1]
    d_b = conv_b.shape[-1]
    d_ff = w_ffn_gate.shape[-1]
    heads = d_a // HEAD_DIM
    assert bsz + 1 <= 8

    lb_all = jnp.cumsum(jax.nn.softmax(lb_logits.astype(F32), axis=0), axis=0)
    lb_all = lb_all - lb_all[0:1]
    lb_flat = lb_all.reshape(depth, 1, 2 * d_a)
    lbvecs = jnp.stack([jnp.log(lb_flat), jnp.log1p(-lb_flat)], axis=1)

    cond = jnp.zeros((8, d_model), F32).at[:bsz].set(c).at[bsz].set(c_ctx)
    mod = _modvec(cond, w_mod, b_mod)

    ff_pad = (-d_ff) % 1024
    wb = lambda w: w.astype(BF16)
    w_in_b, w_a_b, w_b_b, w_o_b = wb(w_in), wb(w_a), wb(w_b), wb(w_o)
    w_gate_b = jnp.pad(wb(w_ffn_gate), ((0, 0), (0, 0), (0, ff_pad)))
    w_up_b = jnp.pad(wb(w_ffn_up), ((0, 0), (0, 0), (0, ff_pad)))
    w_down_b = jnp.pad(wb(w_ffn_down), ((0, 0), (0, ff_pad), (0, 0)))

    pos = _pos_tables(seq_len, d_model)
    xl = x.reshape(bsz * seq_len, d_model)
    xc = ctx.reshape(bsz * ctx_len, d_model)
    s_zero = jnp.zeros((bsz, heads, HEAD_DIM, HEAD_DIM), F32)
    gate_col0 = 3 * d_a

    for layer in range(depth):
        last = layer == depth - 1
        mods = mod[layer].reshape(8, 6, d_model)
        lat = [mods[:bsz, k][:, None, :] for k in range(6)]
        cm = [jnp.broadcast_to(mods[bsz, k][None, None, :], (bsz, 1, d_model)) for k in range(6)]
        sh1, sc1, gt1, sh2, sc2, gt2 = lat
        sh1c, sc1c, gt1c, sh2c, sc2c, gt2c = cm

        if layer == 0:
            h, xl = _modulate(xl, g_norm1[layer], sh1, sc1, seq_len, pos=pos)
        else:
            h = _modulate(xl, g_norm1[layer], sh1, sc1, seq_len)
        hc = _modulate(xc, g_norm1[layer], sh1c, sc1c, ctx_len)

        z1c, gfc, kfc = _mixer_inputs(hc, w_in_b, layer, lbvecs[layer], d_a, d_b, d_model)
        oc_fw, s_fw = _scan(z1c, gfc, kfc, s_zero, bsz, ctx_len, d_a, reverse=False)
        ac, s_bw = _scan(z1c, gfc, kfc, s_zero, bsz, ctx_len, d_a, reverse=True,
                         fin=(oc_fw, g_onorm[layer]))

        z1, gf, kf = _mixer_inputs(h, w_in_b, layer, lbvecs[layer], d_a, d_b, d_model)
        o_fw, _ = _scan(z1, gf, kf, s_fw, bsz, seq_len, d_a, reverse=False)
        a_in, _ = _scan(z1, gf, kf, s_bw, bsz, seq_len, d_a, reverse=True, fin=(o_fw, g_onorm[layer]))

        hglu = _inproj_glu(h, w_in_b, layer, d_a, d_b)
        b_in = _conv_module(hglu, conv_w[layer], conv_b[layer], ln_g[layer], ln_b[layer], bsz, seq_len)
        mrg = _merge(a_in, b_in, w_a_b, w_b_b, layer, z1, gate_col0)
        xl = _proj_residual(mrg, w_o_b, layer, xl, gt1, seq_len, name="out_proj")
        h2 = _modulate(xl, g_norm2[layer], sh2, sc2, seq_len)
        u = _ffn_up(h2, w_gate_b, w_up_b, layer)
        xl = _proj_residual(u, w_down_b, layer, xl, gt2, seq_len, bm_pref=512, name="ffn_down")

        if not last:
            hgluc = _inproj_glu(hc, w_in_b, layer, d_a, d_b)
            bc = _conv_module(hgluc, conv_w[layer], conv_b[layer], ln_g[layer], ln_b[layer], bsz, ctx_len)
            mrgc = _merge(ac, bc, w_a_b, w_b_b, layer, z1c, gate_col0)
            xc = _proj_residual(mrgc, w_o_b, layer, xc, gt1c, ctx_len, name="out_proj_ctx")
            h2c = _modulate(xc, g_norm2[layer], sh2c, sc2c, ctx_len)
            uc = _ffn_up(h2c, w_gate_b, w_up_b, layer)
            xc = _proj_residual(uc, w_down_b, layer, xc, gt2c, ctx_len, bm_pref=512, name="ffn_down_ctx")

    return _final_norm(xl, g_final).reshape(bsz, seq_len, d_model)
```

```python
import functools

import jax
import jax.numpy as jnp
from jax import lax
from jax.experimental import pallas as pl
from jax.experimental.pallas import tpu as pltpu

EPS = 1e-6
GRID_W = 64
POS_BASE = 10000.0
HEAD_DIM = 128
CHUNK = 64
HALF = CHUNK // 2
SUB = 16
SAFE_STEP_LOG_DECAY = 75.0 / (SUB // 2)
CONV_HALO = 16
LANES = 128
ROW_TILE = 2048
ROW_SUB = 512
MIB = 1 << 20
VMEM_CAP = 60 * MIB

F32 = jnp.float32
BF16 = jnp.bfloat16


def _tile(n, pref, mult):
    t = (min(pref, n) // mult) * mult
    while t >= mult:
        if n % t == 0:
            return t
        t -= mult
    raise ValueError(f"no tile for {n} (pref {pref}, mult {mult})")


def _params(sem, vmem_bytes):
    return pltpu.CompilerParams(dimension_semantics=sem,
                                vmem_limit_bytes=int(min(max(vmem_bytes, 16 * MIB), VMEM_CAP)))


def _sigmoid(x):
    return jax.nn.sigmoid(x)


def _silu(x):
    return x * jax.nn.sigmoid(x)


def _log_sigmoid(x):
    return jnp.minimum(x, 0.0) - jnp.log1p(jnp.exp(-jnp.abs(x)))


def _modvec_kernel(a_ref, w_ref, b_ref, o_ref):
    a = a_ref[...]
    act = (a * _sigmoid(a)).astype(BF16)
    o_ref[...] = jnp.dot(act, w_ref[...].astype(BF16), preferred_element_type=F32) + b_ref[...]


def _modvec(cond_rows, w_mod, b_mod):
    depth, d, md = w_mod.shape
    bn = _tile(md, 512, LANES)
    return pl.pallas_call(
        _modvec_kernel,
        grid=(depth, md // bn),
        in_specs=[pl.BlockSpec((8, d), lambda l, j: (0, 0)),
                  pl.BlockSpec((None, d, bn), lambda l, j: (l, 0, j)),
                  pl.BlockSpec((None, 1, bn), lambda l, j: (l, 0, j))],
        out_specs=pl.BlockSpec((None, 8, bn), lambda l, j: (l, 0, j)),
        out_shape=jax.ShapeDtypeStruct((depth, 8, md), F32),
        compiler_params=_params(("arbitrary", "arbitrary"), 2 * d * bn * 4 + d * bn * 2 + 4 * MIB),
        name="modvec",
    )(cond_rows, w_mod, b_mod.reshape(depth, 1, md))


def _modulate_kernel(*refs, with_pos, bm):
    if with_pos:
        x_ref, g_ref, sh_ref, sc_ref, prow_ref, pcol_ref, h_ref, xo_ref = refs
        half = x_ref.shape[1] // 2
        for r in range(bm // GRID_W):
            rows = slice(r * GRID_W, (r + 1) * GRID_W)
            xo_ref[rows, :half] = x_ref[rows, :half] + prow_ref[r:r + 1, :]
            xo_ref[rows, half:] = x_ref[rows, half:] + pcol_ref[...]
        x = xo_ref[...]
    else:
        x_ref, g_ref, sh_ref, sc_ref, h_ref = refs
        x = x_ref[...]
    ms = jnp.mean(x * x, axis=-1, keepdims=True)
    y = x * lax.rsqrt(ms + EPS) * g_ref[...]
    h_ref[...] = (y * (1.0 + sc_ref[...]) + sh_ref[...]).astype(h_ref.dtype)


def _modulate(x2d, gain, shift, scale, seq_len, pos=None):
    m, d = x2d.shape
    with_pos = pos is not None
    bm = _tile(seq_len, 256, GRID_W if with_pos else 8)
    tpb = seq_len // bm
    in_specs = [pl.BlockSpec((bm, d), lambda i: (i, 0)),
                pl.BlockSpec((1, d), lambda i: (0, 0)),
                pl.BlockSpec((None, 1, d), lambda i: (i // tpb, 0, 0)),
                pl.BlockSpec((None, 1, d), lambda i: (i // tpb, 0, 0))]
    args = [x2d, gain.reshape(1, d), shift, scale]
    out_specs = [pl.BlockSpec((bm, d), lambda i: (i, 0))]
    out_shape = [jax.ShapeDtypeStruct((m, d), BF16)]
    if with_pos:
        prow, pcol = pos
        rpt = bm // GRID_W
        in_specs += [pl.BlockSpec((None, rpt, d // 2), lambda i: (i % tpb, 0, 0)),
                     pl.BlockSpec((GRID_W, d // 2), lambda i: (0, 0))]
        args += [prow.reshape(tpb, rpt, d // 2), pcol]
        out_specs.append(pl.BlockSpec((bm, d), lambda i: (i, 0)))
        out_shape.append(jax.ShapeDtypeStruct((m, d), F32))
    out = pl.pallas_call(
        functools.partial(_modulate_kernel, with_pos=with_pos, bm=bm),
        grid=(m // bm,),
        in_specs=in_specs, out_specs=out_specs, out_shape=out_shape,
        compiler_params=_params(("arbitrary",), 2 * bm * d * (4 + 2 + (4 if with_pos else 0)) + 8 * MIB),
        name="modulate_pos" if with_pos else "modulate",
    )(*args)
    return (out[0], out[1]) if with_pos else out[0]


def _final_norm_kernel(x_ref, g_ref, o_ref):
    x = x_ref[...]
    ms = jnp.mean(x * x, axis=-1, keepdims=True)
    o_ref[...] = x * lax.rsqrt(ms + EPS) * g_ref[...]


def _final_norm(x2d, gain):
    m, d = x2d.shape
    bm = _tile(m, 256, 8)
    return pl.pallas_call(
        _final_norm_kernel,
        grid=(m // bm,),
        in_specs=[pl.BlockSpec((bm, d), lambda i: (i, 0)), pl.BlockSpec((1, d), lambda i: (0, 0))],
        out_specs=pl.BlockSpec((bm, d), lambda i: (i, 0)),
        out_shape=jax.ShapeDtypeStruct((m, d), F32),
        compiler_params=_params(("arbitrary",), 4 * bm * d * 4 + 8 * MIB),
        name="final_norm",
    )(x2d, gain.reshape(1, d))


def _matmul_kernel(*refs, n_a, dots, n_extra, body, bm, sub):
    a_refs = refs[:n_a]
    w_refs = refs[n_a:n_a + len(dots)]
    e_refs = refs[n_a + len(dots):n_a + len(dots) + n_extra]
    o_refs = refs[n_a + len(dots) + n_extra:]
    for r in range(bm // sub):
        rows = slice(r * sub, (r + 1) * sub)
        accs = [jnp.dot(a_refs[ai][rows, :], w_refs[d][...], preferred_element_type=F32)
                for d, ai in enumerate(dots)]
        body(accs, e_refs, o_refs, rows)


def _matmul(a_list, dots, extras, outs, body, *, layer, bm, bn, n_col_blocks, name):
    m = a_list[0].shape[0]
    sub = _tile(bm, ROW_SUB, 8)
    assert m % bm == 0
    assert all(w.shape[1] == a_list[ai].shape[1] for (ai, w, _) in dots)
    in_specs = [pl.BlockSpec((bm, a.shape[1]), lambda i, j: (i, 0), pipeline_mode=pl.Buffered(1))
                for a in a_list]
    in_specs += [pl.BlockSpec((None, w.shape[1], bn), (lambda cf: lambda i, j: (layer, 0, cf(j)))(cf))
                 for (_, w, cf) in dots]
    in_specs += [pl.BlockSpec(bs, f) for (_, bs, f) in extras]
    out_specs = [pl.BlockSpec(bs, f) for (_, bs, f) in outs]

    def nbytes(shape, dtype):
        n = 1
        for s in shape:
            n *= (s or 1)
        return n * jnp.dtype(dtype).itemsize

    vmem = sum(bm * a.shape[1] * 2 for a in a_list) + 2 * sum(w.shape[1] * bn * 2 for (_, w, _) in dots)
    vmem += 2 * sum(nbytes(bs, arr.dtype) for (arr, bs, _) in extras)
    vmem += 2 * sum(nbytes(bs, sd.dtype) for (sd, bs, _) in outs)
    vmem += 3 * len(dots) * sub * bn * 4 + 6 * MIB
    return pl.pallas_call(
        functools.partial(_matmul_kernel, n_a=len(a_list), dots=[ai for (ai, _, _) in dots],
                          n_extra=len(extras), body=body, bm=bm, sub=sub),
        grid=(m // bm, n_col_blocks), in_specs=in_specs, out_specs=out_specs,
        out_shape=[sd for (sd, _, _) in outs],
        compiler_params=_params(("arbitrary", "arbitrary"), vmem),
        name=name,
    )(*a_list, *[w for (_, w, _) in dots], *[arr for (arr, _, _) in extras])


def _inproj_main(h, w_in, layer, d_a, d_b, d_model):
    m = h.shape[0]
    bm = _tile(m, ROW_TILE, 8)
    bn = _tile(d_a, 1024, LANES)
    groups = [(0, d_a, "silu"), (d_a, d_a, "id"), (4 * d_a, d_a, "silu"),
              (5 * d_a + 2 * d_b, 2 * d_model, "sigmoid")]
    ranges, j0 = [], 0
    for (src, width, act) in groups:
        assert src % bn == 0 and width % bn == 0
        ranges.append((j0, j0 + width // bn, src // bn - j0, act))
        j0 += width // bn
    n_blocks = j0

    def col_fn(j):
        blk = j + ranges[0][2]
        for (a0, _, off, _) in ranges[1:]:
            blk = jnp.where(j >= a0, j + off, blk)
        return blk

    def body(accs, e_refs, o_refs, rows):
        j = pl.program_id(1)
        x = accs[0]
        s = _sigmoid(x)
        out = x * s
        for (a0, a1, _, act) in ranges:
            if act != "silu":
                out = jnp.where((j >= a0) & (j < a1), x if act == "id" else s, out)
        o_refs[0][rows, :] = out.astype(BF16)

    n_out = n_blocks * bn
    return _matmul([h], [(0, w_in, col_fn)], [],
                   [(jax.ShapeDtypeStruct((m, n_out), BF16), (bm, bn), lambda i, j: (i, j))],
                   body, layer=layer, bm=bm, bn=bn, n_col_blocks=n_blocks, name="inproj_main")[0]


def _inproj_forget(h, w_in, layer, lbvec, d_a):
    m = h.shape[0]
    bm = _tile(m, ROW_TILE, 8)
    bn = _tile(d_a, 512, LANES)
    off = (2 * d_a) // bn

    def body(accs, e_refs, o_refs, rows):
        x = accs[0]
        log_lb, log_1m_lb = e_refs[0][0], e_refs[0][1]
        a = jnp.broadcast_to(log_lb, x.shape)
        b = log_1m_lb + (jnp.minimum(x, 0.0) - jnp.log(1.0 + jnp.exp(-jnp.abs(x))))
        hi = jnp.maximum(a, b)
        log_f = hi + jnp.log(1.0 + jnp.exp(-jnp.abs(a - b)))
        o_refs[0][rows, :] = log_f
        o_refs[1][rows, :] = (1.0 - jnp.exp(log_f)).astype(BF16)

    return _matmul([h], [(0, w_in, lambda j: j + off)],
                   [(lbvec, (2, 1, bn), lambda i, j: (0, 0, j))],
                   [(jax.ShapeDtypeStruct((m, 2 * d_a), F32), (bm, bn), lambda i, j: (i, j)),
                    (jax.ShapeDtypeStruct((m, 2 * d_a), BF16), (bm, bn), lambda i, j: (i, j))],
                   body, layer=layer, bm=bm, bn=bn, n_col_blocks=(2 * d_a) // bn, name="inproj_forget")


def _inproj_glu(h, w_in, layer, d_a, d_b):
    m = h.shape[0]
    bm = _tile(m, ROW_TILE, 8)
    bn = _tile(d_b, 512, LANES)
    off_a, off_g = (5 * d_a) // bn, (5 * d_a + d_b) // bn

    def body(accs, e_refs, o_refs, rows):
        o_refs[0][rows, :] = (accs[0] * _sigmoid(accs[1])).astype(BF16)

    return _matmul([h], [(0, w_in, lambda j: j + off_a), (0, w_in, lambda j: j + off_g)], [],
                   [(jax.ShapeDtypeStruct((m, d_b), BF16), (bm, bn), lambda i, j: (i, j))],
                   body, layer=layer, bm=bm, bn=bn, n_col_blocks=d_b // bn, name="inproj_glu")[0]


def _merge(a_in, b_in, w_a, w_b, layer, z1, gate_col0):
    m, d_model = a_in.shape[0], w_a.shape[2]
    bm = _tile(m, ROW_TILE, 8)
    bn = _tile(d_model, 512, LANES)
    ga0 = gate_col0 // bn
    gb0 = (gate_col0 + d_model) // bn

    def body(accs, e_refs, o_refs, rows):
        ga = e_refs[0][rows, :].astype(F32)
        gb = e_refs[1][rows, :].astype(F32)
        o_refs[0][rows, :] = (ga * accs[0] + gb * accs[1]).astype(BF16)

    return _matmul([a_in, b_in], [(0, w_a, lambda j: j), (1, w_b, lambda j: j)],
                   [(z1, (bm, bn), lambda i, j: (i, j + ga0)), (z1, (bm, bn), lambda i, j: (i, j + gb0))],
                   [(jax.ShapeDtypeStruct((m, d_model), BF16), (bm, bn), lambda i, j: (i, j))],
                   body, layer=layer, bm=bm, bn=bn, n_col_blocks=d_model // bn, name="merge")[0]


def _proj_residual(a, w, layer, x2d, gate, seq_len, *, bm_pref=ROW_TILE, name):
    m, d_model = x2d.shape
    bm = _tile(seq_len, bm_pref, 8)
    bn = _tile(d_model, 512, LANES)
    tpb = seq_len // bm

    def body(accs, e_refs, o_refs, rows):
        o_refs[0][rows, :] = e_refs[0][rows, :] + e_refs[1][...] * accs[0]

    return _matmul([a], [(0, w, lambda j: j)],
                   [(x2d, (bm, bn), lambda i, j: (i, j)), (gate, (None, 1, bn), lambda i, j: (i // tpb, 0, j))],
                   [(jax.ShapeDtypeStruct((m, d_model), F32), (bm, bn), lambda i, j: (i, j))],
                   body, layer=layer, bm=bm, bn=bn, n_col_blocks=d_model // bn, name=name)[0]


def _ffn_up(h, w_gate, w_up, layer):
    m, d_ff = h.shape[0], w_gate.shape[2]
    bm = _tile(m, ROW_TILE, 8)
    bn = _tile(d_ff, 512, LANES)

    def body(accs, e_refs, o_refs, rows):
        o_refs[0][rows, :] = (_silu(accs[0]) * accs[1]).astype(BF16)

    return _matmul([h], [(0, w_gate, lambda j: j), (0, w_up, lambda j: j)], [],
                   [(jax.ShapeDtypeStruct((m, d_ff), BF16), (bm, bn), lambda i, j: (i, j))],
                   body, layer=layer, bm=bm, bn=bn, n_col_blocks=d_ff // bn, name="ffn_up")[0]


def _cumsum_rows(x, reverse):
    n, grp = x.shape[0], 8
    pos = lax.broadcasted_iota(jnp.int32, x.shape, 0) % grp
    s = 1
    while s < grp:
        if reverse:
            x = x + jnp.where(pos < grp - s, pltpu.roll(x, n - s, 0), 0.0)
        else:
            x = x + jnp.where(pos >= s, pltpu.roll(x, s, 0), 0.0)
        s *= 2
    groups = [x[i * grp:(i + 1) * grp] for i in range(n // grp)]
    total_row = 0 if reverse else grp - 1
    order = range(len(groups) - 2, -1, -1) if reverse else range(1, len(groups))
    off = None
    for i in order:
        prev = groups[i + 1] if reverse else groups[i - 1]
        off = jnp.broadcast_to(prev[total_row:total_row + 1, :], (grp, x.shape[1]))
        groups[i] = groups[i] + off
    return jnp.concatenate(groups, axis=0)


_NT = (((1,), (1,)), ((), ()))
_TN = (((0,), (0,)), ((), ()))


def _scan_refs(b, reverse):
    n_sub = CHUNK // SUB
    bc = lambda r: jnp.broadcast_to(b[r:r + 1, :], (SUB, b.shape[1]))
    zero = jnp.zeros((SUB, b.shape[1]), F32)
    if reverse:
        start = [bc(SUB * (a + 1)) for a in range(n_sub - 1)] + [zero]
        stop = [bc(SUB * a) for a in range(n_sub)]
        mid = [bc(SUB * a + SUB // 2) for a in range(n_sub)]
        rho, b_end = b[HALF:HALF + 1, :], b[0:1, :]
    else:
        start = [zero] + [bc(SUB * a - 1) for a in range(1, n_sub)]
        stop = [bc(SUB * a + SUB - 1) for a in range(n_sub)]
        mid = [bc(SUB * a + SUB // 2 - 1) for a in range(n_sub)]
        rho, b_end = b[HALF - 1:HALF, :], b[CHUNK - 1:CHUNK, :]
    cat = lambda parts: jnp.concatenate(parts, axis=0)
    return cat(start), cat(stop), cat(mid), rho, b_end


def _scan_kernel(*refs, reverse, finalize, tb, hb):
    n_in = 8 if finalize else 5
    q_ref, v_ref, g_ref, k_ref, s0_ref = refs[:5]
    if finalize:
        ofw_ref, sog_ref, gon_ref = refs[5:8]
    o_ref, sfin_ref = refs[n_in:n_in + 2]
    st_ref, dec_ref = refs[n_in + 2:n_in + 4]
    q0_ref, k0_ref, q1_ref, k1_ref, q2_ref, k2_ref, qs_ref, ks_ref, p_ref = refs[n_in + 4:]
    step = pl.program_id(2)
    n_chunks = tb // CHUNK

    @pl.when(step == 0)
    def _():
        st_ref[...] = s0_ref[...]

    g_min = g_ref[...]
    while g_min.shape[0] > 8:
        half = g_min.shape[0] // 2
        g_min = jnp.minimum(g_min[:half], g_min[half:])
    safe = jnp.min(g_min) > -SAFE_STEP_LOG_DECAY

    tri_r = lax.broadcasted_iota(jnp.int32, (CHUNK, CHUNK), 0)
    tri_c = lax.broadcasted_iota(jnp.int32, (CHUNK, CHUNK), 1)
    causal = (tri_c >= tri_r) if reverse else (tri_c <= tri_r)
    pt = (CHUNK - 1 - tri_r) if reverse else tri_r
    ps = (CHUNK - 1 - tri_c) if reverse else tri_c
    sub_t, sub_s = pt // SUB, ps // SUB
    m0 = (sub_t == sub_s) & (ps <= pt)
    m1 = (sub_t == sub_s + 1) & (sub_t % 2 == 1)
    m2 = (pt >= HALF) & (ps < HALF)
    end_row = 0 if reverse else CHUNK - 1

    def emit(rows, lanes, hh, o):
        if finalize:
            tot = o + ofw_ref[rows, lanes]
            ms = jnp.mean(tot * tot, axis=-1, keepdims=True)
            y = tot * lax.rsqrt(ms + EPS) * gon_ref[:, lanes]
            o_ref[rows, lanes] = (y * sog_ref[rows, lanes].astype(F32)).astype(o_ref.dtype)
        else:
            o_ref[rows, lanes] = o

    def tile(c, hh):
        return slice(c * CHUNK, (c + 1) * CHUNK), slice(hh * HEAD_DIM, (hh + 1) * HEAD_DIM)

    def prepare(c, hh):
        rows, lanes = tile(c, hh)
        b = _cumsum_rows(g_ref[rows, lanes], reverse)
        q = q_ref[rows, lanes].astype(F32)
        k = k_ref[rows, lanes].astype(F32)
        start, stop, mid, rho, b_end = _scan_refs(b, reverse)
        q0_ref[rows, lanes] = (q * jnp.exp(b - mid)).astype(BF16)
        k0_ref[rows, lanes] = (k * jnp.exp(mid - b)).astype(BF16)
        q1_ref[rows, lanes] = (q * jnp.exp(b - start)).astype(BF16)
        k1_ref[rows, lanes] = (k * jnp.exp(stop - b)).astype(BF16)
        e2 = jnp.exp(-jnp.abs(b - rho))
        q2_ref[rows, lanes] = (q * e2).astype(BF16)
        k2_ref[rows, lanes] = (k * e2).astype(BF16)
        qs_ref[rows, lanes] = (q * jnp.exp(b)).astype(BF16)
        ks_ref[rows, lanes] = (k * jnp.exp(b_end - b)).astype(BF16)
        dec_ref[c, :, lanes] = jnp.exp(b_end)

    nt = lambda a, bb: lax.dot_general(a, bb, _NT, preferred_element_type=F32)

    def scores(c, hh):
        rows, lanes = tile(c, hh)
        s0 = nt(q0_ref[rows, lanes], k0_ref[rows, lanes])
        s1 = nt(q1_ref[rows, lanes], k1_ref[rows, lanes])
        s2 = nt(q2_ref[rows, lanes], k2_ref[rows, lanes])
        p_ref[hh, rows, :] = jnp.where(m0, s0, jnp.where(m1, s1, jnp.where(m2, s2, 0.0))).astype(BF16)

    def instance(c, hh):
        rows, lanes = tile(c, hh)
        v = v_ref[rows, lanes]
        st = st_ref[hh]
        o = nt(qs_ref[rows, lanes], st.astype(BF16)) + jnp.dot(p_ref[hh, rows, :], v,
                                                                preferred_element_type=F32)
        upd = lax.dot_general(v, ks_ref[rows, lanes], _TN, preferred_element_type=F32)
        st_ref[hh] = st * dec_ref[c, :, lanes] + upd
        emit(rows, lanes, hh, o)

    def exact_instance(c, hh):
        rows, lanes = tile(c, hh)
        b = _cumsum_rows(g_ref[rows, lanes], reverse)
        q = q_ref[rows, lanes].astype(F32)
        k = k_ref[rows, lanes].astype(F32)
        v = v_ref[rows, lanes]
        st = st_ref[hh]
        b_end = b[end_row:end_row + 1, :]
        row_id = lax.broadcasted_iota(jnp.int32, b.shape, 0)

        def col(s, sc):
            pick = row_id == s
            b_s = jnp.sum(jnp.where(pick, b, 0.0), axis=0, keepdims=True)
            k_s = jnp.sum(jnp.where(pick, k, 0.0), axis=0, keepdims=True)
            d = jnp.exp(jnp.minimum(b - b_s, 0.0))
            cs = jnp.sum(q * d * k_s, axis=-1, keepdims=True)
            return jnp.where(tri_c == s, cs, sc)

        scores = lax.fori_loop(0, CHUNK, col, jnp.zeros((CHUNK, CHUNK), F32))
        o_int = lax.dot_general((q * jnp.exp(b)).astype(BF16), st.astype(BF16), _NT,
                                preferred_element_type=F32)
        upd = lax.dot_general(v, (k * jnp.exp(b_end - b)).astype(BF16), _TN,
                              preferred_element_type=F32)
        p = jnp.where(causal, scores, 0.0).astype(BF16)
        o = o_int + jnp.dot(p, v, preferred_element_type=F32)
        st_ref[hh] = st * jnp.exp(b_end) + upd
        emit(rows, lanes, hh, o)

    def run(fast):
        order = range(n_chunks - 1, -1, -1) if reverse else range(n_chunks)
        if fast:
            for stage in (prepare, scores):
                for c in order:
                    for hh in range(hb):
                        stage(c, hh)
        for c in order:
            for hh in range(hb):
                (instance if fast else exact_instance)(c, hh)

    @pl.when(safe)
    def _():
        run(True)

    @pl.when(jnp.logical_not(safe))
    def _():
        run(False)

    @pl.when(step == pl.num_programs(2) - 1)
    def _():
        sfin_ref[...] = st_ref[...]


def _scan(z1, gf, kf, s0, nb, seq_len, d_a, *, reverse, fin=None):
    heads = d_a // HEAD_DIM
    hb = 4 if heads % 4 == 0 else 1
    width = hb * HEAD_DIM
    tb = _tile(seq_len, 256, CHUNK)
    nt = seq_len // tb
    ng = heads // hb
    gcols = d_a // width
    dirn = 1 if reverse else 0

    def row(b, i):
        return b * nt + ((nt - 1 - i) if reverse else i)

    in_specs = [pl.BlockSpec((tb, width), lambda b, h, i: (row(b, i), h)),
                pl.BlockSpec((tb, width), lambda b, h, i: (row(b, i), gcols + h)),
                pl.BlockSpec((tb, width), lambda b, h, i: (row(b, i), dirn * gcols + h)),
                pl.BlockSpec((tb, width), lambda b, h, i: (row(b, i), dirn * gcols + h)),
                pl.BlockSpec((None, hb, HEAD_DIM, HEAD_DIM), lambda b, h, i: (b, h, 0, 0))]
    args = [z1, z1, gf, kf, s0]
    if fin is not None:
        o_fw, g_onorm = fin
        in_specs += [pl.BlockSpec((tb, width), lambda b, h, i: (row(b, i), h)),
                     pl.BlockSpec((tb, width), lambda b, h, i: (row(b, i), 2 * gcols + h)),
                     pl.BlockSpec((1, width), lambda b, h, i: (0, h))]
        args += [o_fw, z1, g_onorm.reshape(1, d_a)]
    out_dtype = BF16 if fin is not None else F32
    out_specs = [pl.BlockSpec((tb, width), lambda b, h, i: (row(b, i), h)),
                 pl.BlockSpec((None, hb, HEAD_DIM, HEAD_DIM), lambda b, h, i: (b, h, 0, 0))]
    out_shape = [jax.ShapeDtypeStruct((nb * seq_len, d_a), out_dtype),
                 jax.ShapeDtypeStruct((nb, heads, HEAD_DIM, HEAD_DIM), F32)]
    return pl.pallas_call(
        functools.partial(_scan_kernel, reverse=reverse, finalize=fin is not None, tb=tb, hb=hb),
        grid=(nb, ng, nt),
        in_specs=in_specs, out_specs=out_specs, out_shape=out_shape,
        scratch_shapes=[pltpu.VMEM((hb, HEAD_DIM, HEAD_DIM), F32),
                        pltpu.VMEM((tb // CHUNK, 1, width), F32)]
                       + [pltpu.VMEM((tb, width), BF16) for _ in range(8)]
                       + [pltpu.VMEM((hb, tb, CHUNK), BF16)],
        compiler_params=_params(("arbitrary", "arbitrary", "arbitrary"), 32 * MIB),
        name="scan_bw" if reverse else "scan_fw",
    )(*args)


def _conv_kernel(cur_ref, prev_ref, next_ref, w_ref, cb_ref, lg_ref, lb_ref, o_ref, ext_ref, acc_ref,
                 *, tb, taps):
    i = pl.program_id(1)
    nt = pl.num_programs(1)
    pad = (taps - 1) // 2
    d_b = cur_ref.shape[1]
    n_lt = d_b // LANES
    zeros = jnp.zeros((CONV_HALO, LANES), F32)
    for lt in range(n_lt):
        lanes = slice(lt * LANES, (lt + 1) * LANES)
        ext_ref[lt, 0:CONV_HALO, :] = jnp.where(i > 0, prev_ref[:, lanes].astype(F32), zeros)
        ext_ref[lt, CONV_HALO:CONV_HALO + tb, :] = cur_ref[:, lanes].astype(F32)
        ext_ref[lt, CONV_HALO + tb:, :] = jnp.where(i < nt - 1, next_ref[:, lanes].astype(F32), zeros)
    rg = 64

    def lane_tile(lt, carry):
        for g in range(tb // rg):
            base = g * rg + CONV_HALO - pad
            acc = jnp.zeros((rg, LANES), F32)
            for j in range(taps):
                acc = acc + w_ref[lt, j:j + 1, :] * ext_ref[lt, base + j:base + j + rg, :]
            acc_ref[lt, g * rg:(g + 1) * rg, :] = acc
        return carry

    lax.fori_loop(0, n_lt, lane_tile, 0)
    tiles = [acc_ref[lt] + cb_ref[:, lt * LANES:(lt + 1) * LANES] for lt in range(n_lt)]
    mu = sum(jnp.sum(t, axis=-1, keepdims=True) for t in tiles) * (1.0 / d_b)
    cen = [t - mu for t in tiles]
    var = sum(jnp.sum(c * c, axis=-1, keepdims=True) for c in cen) * (1.0 / d_b)
    inv = lax.rsqrt(var + EPS)
    for lt in range(n_lt):
        lanes = slice(lt * LANES, (lt + 1) * LANES)
        hn = cen[lt] * inv * lg_ref[:, lanes] + lb_ref[:, lanes]
        o_ref[:, lanes] = _silu(hn).astype(o_ref.dtype)


def _conv_module(hglu, conv_w, conv_b, ln_g, ln_b, nb, seq_len):
    m, d_b = hglu.shape
    taps = conv_w.shape[0]
    assert (taps - 1) // 2 < CONV_HALO
    tb = _tile(seq_len, 256, 64)
    nt = seq_len // tb
    hpb = tb // CONV_HALO
    nh = seq_len // CONV_HALO

    def prev_map(b, i):
        return (b * nh + jnp.maximum(i * hpb - 1, 0), 0)

    def next_map(b, i):
        return (b * nh + jnp.minimum((i + 1) * hpb, nh - 1), 0)

    vec = lambda: pl.BlockSpec((1, d_b), lambda b, i: (0, 0))
    return pl.pallas_call(
        functools.partial(_conv_kernel, tb=tb, taps=taps),
        grid=(nb, nt),
        in_specs=[pl.BlockSpec((tb, d_b), lambda b, i: (b * nt + i, 0)),
                  pl.BlockSpec((CONV_HALO, d_b), prev_map),
                  pl.BlockSpec((CONV_HALO, d_b), next_map),
                  pl.BlockSpec((d_b // LANES, taps, LANES), lambda b, i: (0, 0, 0)),
                  vec(), vec(), vec()],
        out_specs=pl.BlockSpec((tb, d_b), lambda b, i: (b * nt + i, 0)),
        out_shape=jax.ShapeDtypeStruct((m, d_b), BF16),
        scratch_shapes=[pltpu.VMEM((d_b // LANES, tb + 2 * CONV_HALO, LANES), F32),
                        pltpu.VMEM((d_b // LANES, tb, LANES), F32)],
        compiler_params=_params(("arbitrary", "arbitrary"), 12 * tb * d_b * 4 + 8 * MIB),
        name="conv_module",
    )(hglu, hglu, hglu, conv_w.reshape(taps, d_b // LANES, LANES).transpose(1, 0, 2), conv_b.reshape(1, d_b),
      ln_g.reshape(1, d_b), ln_b.reshape(1, d_b))


def _pos_tables(seq_len, dim):
    rows = seq_len // GRID_W
    quarter = dim // 4
    omega = 1.0 / (POS_BASE ** (jnp.arange(quarter, dtype=F32) / quarter))
    ang_r = jnp.arange(rows, dtype=F32)[:, None] * omega
    ang_c = jnp.arange(GRID_W, dtype=F32)[:, None] * omega
    prow = jnp.concatenate([jnp.sin(ang_r), jnp.cos(ang_r)], axis=-1)
    pcol = jnp.concatenate([jnp.sin(ang_c), jnp.cos(ang_c)], axis=-1)
    return prow, pcol


def _mixer_inputs(h, w_in, layer, lbvec, d_a, d_b, d_model):
    z1 = _inproj_main(h, w_in, layer, d_a, d_b, d_model)
    gf, kf = _inproj_forget(h, w_in, layer, lbvec, d_a)
    return z1, gf, kf


def kernel(x, c, ctx, c_ctx, w_mod, b_mod, g_norm1, w_in, lb_logits, g_onorm, w_a, conv_w, conv_b,
           ln_g, ln_b, w_b, w_o, g_norm2, w_ffn_gate, w_ffn_up, w_ffn_down, g_final):
    bsz, seq_len, d_model = x.shape
    ctx_len = ctx.shape[1]
    depth = w_mod.shape[0]
    d_a = lb_logits.shape[-1]
    d_b = conv_b.shape[-1]
    d_ff = w_ffn_gate.shape[-1]
    heads = d_a // HEAD_DIM
    assert bsz + 1 <= 8

    lb_all = jnp.cumsum(jax.nn.softmax(lb_logits.astype(F32), axis=0), axis=0)
    lb_all = lb_all - lb_all[0:1]
    lb_flat = lb_all.reshape(depth, 1, 2 * d_a)
    lbvecs = jnp.stack([jnp.log(lb_flat), jnp.log1p(-lb_flat)], axis=1)

    cond = jnp.zeros((8, d_model), F32).at[:bsz].set(c).at[bsz].set(c_ctx)
    mod = _modvec(cond, w_mod, b_mod)

    ff_pad = (-d_ff) % 1024
    wb = lambda w: w.astype(BF16)
    w_in_b, w_a_b, w_b_b, w_o_b = wb(w_in), wb(w_a), wb(w_b), wb(w_o)
    w_gate_b = wb(jnp.pad(w_ffn_gate, ((0, 0), (0, 0), (0, ff_pad))))
    w_up_b = wb(jnp.pad(w_ffn_up, ((0, 0), (0, 0), (0, ff_pad))))
    w_down_b = wb(jnp.pad(w_ffn_down, ((0, 0), (0, ff_pad), (0, 0))))

    pos = _pos_tables(seq_len, d_model)
    xl = x.reshape(bsz * seq_len, d_model)
    xc = ctx.reshape(bsz * ctx_len, d_model)
    s_zero = jnp.zeros((bsz, heads, HEAD_DIM, HEAD_DIM), F32)
    gate_col0 = 3 * d_a

    for layer in range(depth):
        last = layer == depth - 1
        mods = mod[layer].reshape(8, 6, d_model)
        lat = [mods[:bsz, k][:, None, :] for k in range(6)]
        cm = [jnp.broadcast_to(mods[bsz, k][None, None, :], (bsz, 1, d_model)) for k in range(6)]
        sh1, sc1, gt1, sh2, sc2, gt2 = lat
        sh1c, sc1c, gt1c, sh2c, sc2c, gt2c = cm

        if layer == 0:
            h, xl = _modulate(xl, g_norm1[layer], sh1, sc1, seq_len, pos=pos)
        else:
            h = _modulate(xl, g_norm1[layer], sh1, sc1, seq_len)
        hc = _modulate(xc, g_norm1[layer], sh1c, sc1c, ctx_len)

        z1c, gfc, kfc = _mixer_inputs(hc, w_in_b, layer, lbvecs[layer], d_a, d_b, d_model)
        oc_fw, s_fw = _scan(z1c, gfc, kfc, s_zero, bsz, ctx_len, d_a, reverse=False)
        ac, s_bw = _scan(z1c, gfc, kfc, s_zero, bsz, ctx_len, d_a, reverse=True,
                         fin=(oc_fw, g_onorm[layer]))

        z1, gf, kf = _mixer_inputs(h, w_in_b, layer, lbvecs[layer], d_a, d_b, d_model)
        o_fw, _ = _scan(z1, gf, kf, s_fw, bsz, seq_len, d_a, reverse=False)
        a_in, _ = _scan(z1, gf, kf, s_bw, bsz, seq_len, d_a, reverse=True, fin=(o_fw, g_onorm[layer]))

        hglu = _inproj_glu(h, w_in_b, layer, d_a, d_b)
        b_in = _conv_module(hglu, conv_w[layer], conv_b[layer], ln_g[layer], ln_b[layer], bsz, seq_len)
        mrg = _merge(a_in, b_in, w_a_b, w_b_b, layer, z1, gate_col0)
        xl = _proj_residual(mrg, w_o_b, layer, xl, gt1, seq_len, name="out_proj")
        h2 = _modulate(xl, g_norm2[layer], sh2, sc2, seq_len)
        u = _ffn_up(h2, w_gate_b, w_up_b, layer)
        xl = _proj_residual(u, w_down_b, layer, xl, gt2, seq_len, bm_pref=512, name="ffn_down")

        if not last:
            hgluc = _inproj_glu(hc, w_in_b, layer, d_a, d_b)
            bc = _conv_module(hgluc, conv_w[layer], conv_b[layer], ln_g[layer], ln_b[layer], bsz, ctx_len)
            mrgc = _merge(ac, bc, w_a_b, w_b_b, layer, z1c, gate_col0)
            xc = _proj_residual(mrgc, w_o_b, layer, xc, gt1c, ctx_len, name="out_proj_ctx")
            h2c = _modulate(xc, g_norm2[layer], sh2c, sc2c, ctx_len)
            uc = _ffn_up(h2c, w_gate_b, w_up_b, layer)
            xc = _proj_residual(uc, w_down_b, layer, xc, gt2c, ctx_len, bm_pref=512, name="ffn_down_ctx")

    return _final_norm(xl, g_final).reshape(bsz, seq_len, d_model)
```

```python
import functools

import jax
import jax.numpy as jnp
from jax import lax
from jax.experimental import pallas as pl
from jax.experimental.pallas import tpu as pltpu

EPS = 1e-6
GRID_W = 64
POS_BASE = 10000.0
HEAD_DIM = 128
CHUNK = 64
HALF = CHUNK // 2
SUB = 16
SAFE_STEP_LOG_DECAY = 75.0 / (SUB // 2)
CONV_HALO = 16
LANES = 128
ROW_TILE = 1024
ROW_SUB = 1024
MIB = 1 << 20
VMEM_CAP = 60 * MIB

F32 = jnp.float32
BF16 = jnp.bfloat16


def _tile(n, pref, mult):
    t = (min(pref, n) // mult) * mult
    while t >= mult:
        if n % t == 0:
            return t
        t -= mult
    raise ValueError(f"no tile for {n} (pref {pref}, mult {mult})")


def _params(sem, vmem_bytes):
    return pltpu.CompilerParams(dimension_semantics=sem,
                                vmem_limit_bytes=int(min(max(vmem_bytes, 16 * MIB), VMEM_CAP)))


def _sigmoid(x):
    return jax.nn.sigmoid(x)


def _silu(x):
    return x * jax.nn.sigmoid(x)


def _log_sigmoid(x):
    return jnp.minimum(x, 0.0) - jnp.log1p(jnp.exp(-jnp.abs(x)))


def _modvec_kernel(a_ref, w_ref, b_ref, o_ref):
    a = a_ref[...]
    act = (a * _sigmoid(a)).astype(BF16)
    o_ref[...] = jnp.dot(act, w_ref[...].astype(BF16), preferred_element_type=F32) + b_ref[...]


def _modvec(cond_rows, w_mod, b_mod):
    depth, d, md = w_mod.shape
    bn = _tile(md, 512, LANES)
    return pl.pallas_call(
        _modvec_kernel,
        grid=(depth, md // bn),
        in_specs=[pl.BlockSpec((8, d), lambda l, j: (0, 0)),
                  pl.BlockSpec((None, d, bn), lambda l, j: (l, 0, j)),
                  pl.BlockSpec((None, 1, bn), lambda l, j: (l, 0, j))],
        out_specs=pl.BlockSpec((None, 8, bn), lambda l, j: (l, 0, j)),
        out_shape=jax.ShapeDtypeStruct((depth, 8, md), F32),
        compiler_params=_params(("arbitrary", "arbitrary"), 2 * d * bn * 4 + d * bn * 2 + 4 * MIB),
        name="modvec",
    )(cond_rows, w_mod, b_mod.reshape(depth, 1, md))


def _modulate_kernel(*refs, with_pos, bm):
    if with_pos:
        x_ref, g_ref, sh_ref, sc_ref, prow_ref, pcol_ref, h_ref, xo_ref = refs
        half = x_ref.shape[1] // 2
        for r in range(bm // GRID_W):
            rows = slice(r * GRID_W, (r + 1) * GRID_W)
            xo_ref[rows, :half] = x_ref[rows, :half] + prow_ref[r:r + 1, :]
            xo_ref[rows, half:] = x_ref[rows, half:] + pcol_ref[...]
        x = xo_ref[...]
    else:
        x_ref, g_ref, sh_ref, sc_ref, h_ref = refs
        x = x_ref[...]
    ms = jnp.mean(x * x, axis=-1, keepdims=True)
    y = x * lax.rsqrt(ms + EPS) * g_ref[...]
    h_ref[...] = (y * (1.0 + sc_ref[...]) + sh_ref[...]).astype(h_ref.dtype)


def _modulate(x2d, gain, shift, scale, seq_len, pos=None):
    m, d = x2d.shape
    with_pos = pos is not None
    bm = _tile(seq_len, 256, GRID_W if with_pos else 8)
    tpb = seq_len // bm
    in_specs = [pl.BlockSpec((bm, d), lambda i: (i, 0)),
                pl.BlockSpec((1, d), lambda i: (0, 0)),
                pl.BlockSpec((None, 1, d), lambda i: (i // tpb, 0, 0)),
                pl.BlockSpec((None, 1, d), lambda i: (i // tpb, 0, 0))]
    args = [x2d, gain.reshape(1, d), shift, scale]
    out_specs = [pl.BlockSpec((bm, d), lambda i: (i, 0))]
    out_shape = [jax.ShapeDtypeStruct((m, d), BF16)]
    if with_pos:
        prow, pcol = pos
        rpt = bm // GRID_W
        in_specs += [pl.BlockSpec((None, rpt, d // 2), lambda i: (i % tpb, 0, 0)),
                     pl.BlockSpec((GRID_W, d // 2), lambda i: (0, 0))]
        args += [prow.reshape(tpb, rpt, d // 2), pcol]
        out_specs.append(pl.BlockSpec((bm, d), lambda i: (i, 0)))
        out_shape.append(jax.ShapeDtypeStruct((m, d), F32))
    out = pl.pallas_call(
        functools.partial(_modulate_kernel, with_pos=with_pos, bm=bm),
        grid=(m // bm,),
        in_specs=in_specs, out_specs=out_specs, out_shape=out_shape,
        compiler_params=_params(("arbitrary",), 2 * bm * d * (4 + 2 + (4 if with_pos else 0)) + 8 * MIB),
        name="modulate_pos" if with_pos else "modulate",
    )(*args)
    return (out[0], out[1]) if with_pos else out[0]


def _final_norm_kernel(x_ref, g_ref, o_ref):
    x = x_ref[...]
    ms = jnp.mean(x * x, axis=-1, keepdims=True)
    o_ref[...] = x * lax.rsqrt(ms + EPS) * g_ref[...]


def _final_norm(x2d, gain):
    m, d = x2d.shape
    bm = _tile(m, 256, 8)
    return pl.pallas_call(
        _final_norm_kernel,
        grid=(m // bm,),
        in_specs=[pl.BlockSpec((bm, d), lambda i: (i, 0)), pl.BlockSpec((1, d), lambda i: (0, 0))],
        out_specs=pl.BlockSpec((bm, d), lambda i: (i, 0)),
        out_shape=jax.ShapeDtypeStruct((m, d), F32),
        compiler_params=_params(("arbitrary",), 4 * bm * d * 4 + 8 * MIB),
        name="final_norm",
    )(x2d, gain.reshape(1, d))


def _cast_pad_kernel(x_ref, o_ref, *, n_valid):
    keep = pl.program_id(1) < n_valid
    o_ref[...] = jnp.where(keep, x_ref[...], 0.0).astype(o_ref.dtype)


def _cast_pad(w, axis, new_size):
    depth, kdim, ndim = w.shape
    old = w.shape[axis]
    blk = _tile(old, 256, LANES)
    assert new_size % blk == 0
    n_valid, n_total = old // blk, new_size // blk
    if axis == 2:
        block = (None, kdim, blk)
        in_map = lambda l, j: (l, 0, jnp.minimum(j, n_valid - 1))
        out_map = lambda l, j: (l, 0, j)
        out_shape = (depth, kdim, new_size)
    else:
        block = (None, blk, ndim)
        in_map = lambda l, j: (l, jnp.minimum(j, n_valid - 1), 0)
        out_map = lambda l, j: (l, j, 0)
        out_shape = (depth, new_size, ndim)
    return pl.pallas_call(
        functools.partial(_cast_pad_kernel, n_valid=n_valid),
        grid=(depth, n_total),
        in_specs=[pl.BlockSpec(block, in_map)],
        out_specs=pl.BlockSpec(block, out_map),
        out_shape=jax.ShapeDtypeStruct(out_shape, BF16),
        compiler_params=_params(("arbitrary", "arbitrary"), 12 * (kdim if axis == 2 else ndim) * blk + 4 * MIB),
        name="cast_pad",
    )(w)


def _matmul_kernel(*refs, n_a, dots, n_extra, body, bm, sub):
    a_refs = refs[:n_a]
    w_refs = refs[n_a:n_a + len(dots)]
    e_refs = refs[n_a + len(dots):n_a + len(dots) + n_extra]
    o_refs = refs[n_a + len(dots) + n_extra:]
    for r in range(bm // sub):
        rows = slice(r * sub, (r + 1) * sub)
        accs = [jnp.dot(a_refs[ai][rows, :], w_refs[d][...], preferred_element_type=F32)
                for d, ai in enumerate(dots)]
        body(accs, e_refs, o_refs, rows)


def _matmul(a_list, dots, extras, outs, body, *, layer, bm, bn, n_col_blocks, name):
    m = a_list[0].shape[0]
    sub = _tile(bm, ROW_SUB, 8)
    assert m % bm == 0
    assert all(w.shape[1] == a_list[ai].shape[1] for (ai, w, _) in dots)
    in_specs = [pl.BlockSpec((bm, a.shape[1]), lambda i, j: (i, 0)) for a in a_list]
    in_specs += [pl.BlockSpec((None, w.shape[1], bn), (lambda cf: lambda i, j: (layer, 0, cf(j)))(cf))
                 for (_, w, cf) in dots]
    in_specs += [pl.BlockSpec(bs, f) for (_, bs, f) in extras]
    out_specs = [pl.BlockSpec(bs, f) for (_, bs, f) in outs]

    def nbytes(shape, dtype):
        n = 1
        for s in shape:
            n *= (s or 1)
        return n * jnp.dtype(dtype).itemsize

    vmem = 2 * (sum(bm * a.shape[1] * 2 for a in a_list) + sum(w.shape[1] * bn * 2 for (_, w, _) in dots))
    vmem += 2 * sum(nbytes(bs, arr.dtype) for (arr, bs, _) in extras)
    vmem += 2 * sum(nbytes(bs, sd.dtype) for (sd, bs, _) in outs)
    vmem += 3 * len(dots) * sub * bn * 4 + 6 * MIB
    return pl.pallas_call(
        functools.partial(_matmul_kernel, n_a=len(a_list), dots=[ai for (ai, _, _) in dots],
                          n_extra=len(extras), body=body, bm=bm, sub=sub),
        grid=(m // bm, n_col_blocks), in_specs=in_specs, out_specs=out_specs,
        out_shape=[sd for (sd, _, _) in outs],
        compiler_params=_params(("arbitrary", "arbitrary"), vmem),
        name=name,
    )(*a_list, *[w for (_, w, _) in dots], *[arr for (arr, _, _) in extras])


def _inproj_main(h, w_in, layer, d_a, d_b, d_model):
    m = h.shape[0]
    bm = _tile(m, ROW_TILE, 8)
    bn = _tile(d_a, 1024, LANES)
    groups = [(0, d_a, "silu"), (d_a, d_a, "id"), (4 * d_a, d_a, "silu"),
              (5 * d_a + 2 * d_b, 2 * d_model, "sigmoid")]
    ranges, j0 = [], 0
    for (src, width, act) in groups:
        assert src % bn == 0 and width % bn == 0
        ranges.append((j0, j0 + width // bn, src // bn - j0, act))
        j0 += width // bn
    n_blocks = j0

    def col_fn(j):
        blk = j + ranges[0][2]
        for (a0, _, off, _) in ranges[1:]:
            blk = jnp.where(j >= a0, j + off, blk)
        return blk

    def body(accs, e_refs, o_refs, rows):
        j = pl.program_id(1)
        x = accs[0]
        s = _sigmoid(x)
        out = x * s
        for (a0, a1, _, act) in ranges:
            if act != "silu":
                out = jnp.where((j >= a0) & (j < a1), x if act == "id" else s, out)
        o_refs[0][rows, :] = out.astype(BF16)

    n_out = n_blocks * bn
    return _matmul([h], [(0, w_in, col_fn)], [],
                   [(jax.ShapeDtypeStruct((m, n_out), BF16), (bm, bn), lambda i, j: (i, j))],
                   body, layer=layer, bm=bm, bn=bn, n_col_blocks=n_blocks, name="inproj_main")[0]


def _inproj_forget(h, w_in, layer, lbvec, d_a):
    m = h.shape[0]
    bm = _tile(m, ROW_TILE, 8)
    bn = _tile(d_a, 512, LANES)
    off = (2 * d_a) // bn

    def body(accs, e_refs, o_refs, rows):
        x = accs[0]
        log_lb, log_1m_lb = e_refs[0][0], e_refs[0][1]
        a = jnp.broadcast_to(log_lb, x.shape)
        b = log_1m_lb + (jnp.minimum(x, 0.0) - jnp.log(1.0 + jnp.exp(-jnp.abs(x))))
        hi = jnp.maximum(a, b)
        log_f = hi + jnp.log(1.0 + jnp.exp(-jnp.abs(a - b)))
        o_refs[0][rows, :] = log_f
        o_refs[1][rows, :] = (1.0 - jnp.exp(log_f)).astype(BF16)

    return _matmul([h], [(0, w_in, lambda j: j + off)],
                   [(lbvec, (2, 1, bn), lambda i, j: (0, 0, j))],
                   [(jax.ShapeDtypeStruct((m, 2 * d_a), F32), (bm, bn), lambda i, j: (i, j)),
                    (jax.ShapeDtypeStruct((m, 2 * d_a), BF16), (bm, bn), lambda i, j: (i, j))],
                   body, layer=layer, bm=bm, bn=bn, n_col_blocks=(2 * d_a) // bn, name="inproj_forget")


def _inproj_glu(h, w_in, layer, d_a, d_b):
    m = h.shape[0]
    bm = _tile(m, ROW_TILE, 8)
    bn = _tile(d_b, 512, LANES)
    off_a, off_g = (5 * d_a) // bn, (5 * d_a + d_b) // bn

    def body(accs, e_refs, o_refs, rows):
        o_refs[0][rows, :] = (accs[0] * _sigmoid(accs[1])).astype(BF16)

    return _matmul([h], [(0, w_in, lambda j: j + off_a), (0, w_in, lambda j: j + off_g)], [],
                   [(jax.ShapeDtypeStruct((m, d_b), BF16), (bm, bn), lambda i, j: (i, j))],
                   body, layer=layer, bm=bm, bn=bn, n_col_blocks=d_b // bn, name="inproj_glu")[0]


def _merge(a_in, b_in, w_a, w_b, layer, z1, gate_col0):
    m, d_model = a_in.shape[0], w_a.shape[2]
    bm = _tile(m, ROW_TILE, 8)
    bn = _tile(d_model, 512, LANES)
    ga0 = gate_col0 // bn
    gb0 = (gate_col0 + d_model) // bn

    def body(accs, e_refs, o_refs, rows):
        ga = e_refs[0][rows, :].astype(F32)
        gb = e_refs[1][rows, :].astype(F32)
        o_refs[0][rows, :] = (ga * accs[0] + gb * accs[1]).astype(BF16)

    return _matmul([a_in, b_in], [(0, w_a, lambda j: j), (1, w_b, lambda j: j)],
                   [(z1, (bm, bn), lambda i, j: (i, j + ga0)), (z1, (bm, bn), lambda i, j: (i, j + gb0))],
                   [(jax.ShapeDtypeStruct((m, d_model), BF16), (bm, bn), lambda i, j: (i, j))],
                   body, layer=layer, bm=bm, bn=bn, n_col_blocks=d_model // bn, name="merge")[0]


def _proj_residual(a, w, layer, x2d, gate, seq_len, *, bm_pref=ROW_TILE, name):
    m, d_model = x2d.shape
    bm = _tile(seq_len, bm_pref, 8)
    bn = _tile(d_model, 512, LANES)
    tpb = seq_len // bm

    def body(accs, e_refs, o_refs, rows):
        o_refs[0][rows, :] = e_refs[0][rows, :] + e_refs[1][...] * accs[0]

    return _matmul([a], [(0, w, lambda j: j)],
                   [(x2d, (bm, bn), lambda i, j: (i, j)), (gate, (None, 1, bn), lambda i, j: (i // tpb, 0, j))],
                   [(jax.ShapeDtypeStruct((m, d_model), F32), (bm, bn), lambda i, j: (i, j))],
                   body, layer=layer, bm=bm, bn=bn, n_col_blocks=d_model // bn, name=name)[0]


def _ffn_up(h, w_gate, w_up, layer):
    m, d_ff = h.shape[0], w_gate.shape[2]
    bm = _tile(m, ROW_TILE, 8)
    bn = _tile(d_ff, 512, LANES)

    def body(accs, e_refs, o_refs, rows):
        o_refs[0][rows, :] = (_silu(accs[0]) * accs[1]).astype(BF16)

    return _matmul([h], [(0, w_gate, lambda j: j), (0, w_up, lambda j: j)], [],
                   [(jax.ShapeDtypeStruct((m, d_ff), BF16), (bm, bn), lambda i, j: (i, j))],
                   body, layer=layer, bm=bm, bn=bn, n_col_blocks=d_ff // bn, name="ffn_up")[0]


def _cumsum_rows(x, reverse):
    n, grp = x.shape[0], 8
    pos = lax.broadcasted_iota(jnp.int32, x.shape, 0) % grp
    s = 1
    while s < grp:
        if reverse:
            x = x + jnp.where(pos < grp - s, pltpu.roll(x, n - s, 0), 0.0)
        else:
            x = x + jnp.where(pos >= s, pltpu.roll(x, s, 0), 0.0)
        s *= 2
    groups = [x[i * grp:(i + 1) * grp] for i in range(n // grp)]
    total_row = 0 if reverse else grp - 1
    order = range(len(groups) - 2, -1, -1) if reverse else range(1, len(groups))
    off = None
    for i in order:
        prev = groups[i + 1] if reverse else groups[i - 1]
        off = jnp.broadcast_to(prev[total_row:total_row + 1, :], (grp, x.shape[1]))
        groups[i] = groups[i] + off
    return jnp.concatenate(groups, axis=0)


_NT = (((1,), (1,)), ((), ()))
_TN = (((0,), (0,)), ((), ()))


def _scan_refs(b, reverse):
    n_sub = CHUNK // SUB
    bc = lambda r: jnp.broadcast_to(b[r:r + 1, :], (SUB, b.shape[1]))
    zero = jnp.zeros((SUB, b.shape[1]), F32)
    if reverse:
        start = [bc(SUB * (a + 1)) for a in range(n_sub - 1)] + [zero]
        stop = [bc(SUB * a) for a in range(n_sub)]
        mid = [bc(SUB * a + SUB // 2) for a in range(n_sub)]
        rho, b_end = b[HALF:HALF + 1, :], b[0:1, :]
    else:
        start = [zero] + [bc(SUB * a - 1) for a in range(1, n_sub)]
        stop = [bc(SUB * a + SUB - 1) for a in range(n_sub)]
        mid = [bc(SUB * a + SUB // 2 - 1) for a in range(n_sub)]
        rho, b_end = b[HALF - 1:HALF, :], b[CHUNK - 1:CHUNK, :]
    cat = lambda parts: jnp.concatenate(parts, axis=0)
    return cat(start), cat(stop), cat(mid), rho, b_end


def _scan_kernel(*refs, reverse, finalize, tb, hb):
    n_in = 8 if finalize else 5
    q_ref, v_ref, g_ref, k_ref, s0_ref = refs[:5]
    if finalize:
        ofw_ref, sog_ref, gon_ref = refs[5:8]
    o_ref, sfin_ref = refs[n_in:n_in + 2]
    st_ref, dec_ref = refs[n_in + 2:n_in + 4]
    q0_ref, k0_ref, q1_ref, k1_ref, q2_ref, k2_ref, qs_ref, ks_ref, p_ref = refs[n_in + 4:]
    step = pl.program_id(2)
    n_chunks = tb // CHUNK

    @pl.when(step == 0)
    def _():
        st_ref[...] = s0_ref[...]

    g_min = g_ref[...]
    while g_min.shape[0] > 8:
        half = g_min.shape[0] // 2
        g_min = jnp.minimum(g_min[:half], g_min[half:])
    safe = jnp.min(g_min) > -SAFE_STEP_LOG_DECAY

    tri_r = lax.broadcasted_iota(jnp.int32, (CHUNK, CHUNK), 0)
    tri_c = lax.broadcasted_iota(jnp.int32, (CHUNK, CHUNK), 1)
    causal = (tri_c >= tri_r) if reverse else (tri_c <= tri_r)
    pt = (CHUNK - 1 - tri_r) if reverse else tri_r
    ps = (CHUNK - 1 - tri_c) if reverse else tri_c
    sub_t, sub_s = pt // SUB, ps // SUB
    m0 = (sub_t == sub_s) & (ps <= pt)
    m1 = (sub_t == sub_s + 1) & (sub_t % 2 == 1)
    m2 = (pt >= HALF) & (ps < HALF)
    end_row = 0 if reverse else CHUNK - 1

    def emit(rows, lanes, hh, o):
        if finalize:
            tot = o + ofw_ref[rows, lanes]
            ms = jnp.mean(tot * tot, axis=-1, keepdims=True)
            y = tot * lax.rsqrt(ms + EPS) * gon_ref[:, lanes]
            o_ref[rows, lanes] = (y * sog_ref[rows, lanes].astype(F32)).astype(o_ref.dtype)
        else:
            o_ref[rows, lanes] = o

    def tile(c, hh):
        return slice(c * CHUNK, (c + 1) * CHUNK), slice(hh * HEAD_DIM, (hh + 1) * HEAD_DIM)

    def prepare(c, hh):
        rows, lanes = tile(c, hh)
        b = _cumsum_rows(g_ref[rows, lanes], reverse)
        q = q_ref[rows, lanes].astype(F32)
        k = k_ref[rows, lanes].astype(F32)
        start, stop, mid, rho, b_end = _scan_refs(b, reverse)
        q0_ref[rows, lanes] = (q * jnp.exp(b - mid)).astype(BF16)
        k0_ref[rows, lanes] = (k * jnp.exp(mid - b)).astype(BF16)
        q1_ref[rows, lanes] = (q * jnp.exp(b - start)).astype(BF16)
        k1_ref[rows, lanes] = (k * jnp.exp(stop - b)).astype(BF16)
        e2 = jnp.exp(-jnp.abs(b - rho))
        q2_ref[rows, lanes] = (q * e2).astype(BF16)
        k2_ref[rows, lanes] = (k * e2).astype(BF16)
        qs_ref[rows, lanes] = (q * jnp.exp(b)).astype(BF16)
        ks_ref[rows, lanes] = (k * jnp.exp(b_end - b)).astype(BF16)
        dec_ref[c, :, lanes] = jnp.exp(b_end)

    nt = lambda a, bb: lax.dot_general(a, bb, _NT, preferred_element_type=F32)

    def scores(c, hh):
        rows, lanes = tile(c, hh)
        s0 = nt(q0_ref[rows, lanes], k0_ref[rows, lanes])
        s1 = nt(q1_ref[rows, lanes], k1_ref[rows, lanes])
        s2 = nt(q2_ref[rows, lanes], k2_ref[rows, lanes])
        p_ref[hh, rows, :] = jnp.where(m0, s0, jnp.where(m1, s1, jnp.where(m2, s2, 0.0))).astype(BF16)

    def instance(c, hh):
        rows, lanes = tile(c, hh)
        v = v_ref[rows, lanes]
        st = st_ref[hh]
        o = nt(qs_ref[rows, lanes], st.astype(BF16)) + jnp.dot(p_ref[hh, rows, :], v,
                                                                preferred_element_type=F32)
        upd = lax.dot_general(v, ks_ref[rows, lanes], _TN, preferred_element_type=F32)
        st_ref[hh] = st * dec_ref[c, :, lanes] + upd
        emit(rows, lanes, hh, o)

    def exact_instance(c, hh):
        rows, lanes = tile(c, hh)
        b = _cumsum_rows(g_ref[rows, lanes], reverse)
        q = q_ref[rows, lanes].astype(F32)
        k = k_ref[rows, lanes].astype(F32)
        v = v_ref[rows, lanes]
        st = st_ref[hh]
        b_end = b[end_row:end_row + 1, :]
        row_id = lax.broadcasted_iota(jnp.int32, b.shape, 0)

        def col(s, sc):
            pick = row_id == s
            b_s = jnp.sum(jnp.where(pick, b, 0.0), axis=0, keepdims=True)
            k_s = jnp.sum(jnp.where(pick, k, 0.0), axis=0, keepdims=True)
            d = jnp.exp(jnp.minimum(b - b_s, 0.0))
            cs = jnp.sum(q * d * k_s, axis=-1, keepdims=True)
            return jnp.where(tri_c == s, cs, sc)

        scores = lax.fori_loop(0, CHUNK, col, jnp.zeros((CHUNK, CHUNK), F32))
        o_int = lax.dot_general((q * jnp.exp(b)).astype(BF16), st.astype(BF16), _NT,
                                preferred_element_type=F32)
        upd = lax.dot_general(v, (k * jnp.exp(b_end - b)).astype(BF16), _TN,
                              preferred_element_type=F32)
        p = jnp.where(causal, scores, 0.0).astype(BF16)
        o = o_int + jnp.dot(p, v, preferred_element_type=F32)
        st_ref[hh] = st * jnp.exp(b_end) + upd
        emit(rows, lanes, hh, o)

    def run(fast):
        order = range(n_chunks - 1, -1, -1) if reverse else range(n_chunks)
        if fast:
            for stage in (prepare, scores):
                for c in order:
                    for hh in range(hb):
                        stage(c, hh)
        for c in order:
            for hh in range(hb):
                (instance if fast else exact_instance)(c, hh)

    @pl.when(safe)
    def _():
        run(True)

    @pl.when(jnp.logical_not(safe))
    def _():
        run(False)

    @pl.when(step == pl.num_programs(2) - 1)
    def _():
        sfin_ref[...] = st_ref[...]


def _scan(z1, gf, kf, s0, nb, seq_len, d_a, *, reverse, fin=None):
    heads = d_a // HEAD_DIM
    hb = 4 if heads % 4 == 0 else 1
    width = hb * HEAD_DIM
    tb = _tile(seq_len, 256, CHUNK)
    nt = seq_len // tb
    ng = heads // hb
    gcols = d_a // width
    dirn = 1 if reverse else 0

    def row(b, i):
        return b * nt + ((nt - 1 - i) if reverse else i)

    in_specs = [pl.BlockSpec((tb, width), lambda b, h, i: (row(b, i), h)),
                pl.BlockSpec((tb, width), lambda b, h, i: (row(b, i), gcols + h)),
                pl.BlockSpec((tb, width), lambda b, h, i: (row(b, i), dirn * gcols + h)),
                pl.BlockSpec((tb, width), lambda b, h, i: (row(b, i), dirn * gcols + h)),
                pl.BlockSpec((None, hb, HEAD_DIM, HEAD_DIM), lambda b, h, i: (b, h, 0, 0))]
    args = [z1, z1, gf, kf, s0]
    if fin is not None:
        o_fw, g_onorm = fin
        in_specs += [pl.BlockSpec((tb, width), lambda b, h, i: (row(b, i), h)),
                     pl.BlockSpec((tb, width), lambda b, h, i: (row(b, i), 2 * gcols + h)),
                     pl.BlockSpec((1, width), lambda b, h, i: (0, h))]
        args += [o_fw, z1, g_onorm.reshape(1, d_a)]
    out_dtype = BF16 if fin is not None else F32
    out_specs = [pl.BlockSpec((tb, width), lambda b, h, i: (row(b, i), h)),
                 pl.BlockSpec((None, hb, HEAD_DIM, HEAD_DIM), lambda b, h, i: (b, h, 0, 0))]
    out_shape = [jax.ShapeDtypeStruct((nb * seq_len, d_a), out_dtype),
                 jax.ShapeDtypeStruct((nb, heads, HEAD_DIM, HEAD_DIM), F32)]
    return pl.pallas_call(
        functools.partial(_scan_kernel, reverse=reverse, finalize=fin is not None, tb=tb, hb=hb),
        grid=(nb, ng, nt),
        in_specs=in_specs, out_specs=out_specs, out_shape=out_shape,
        scratch_shapes=[pltpu.VMEM((hb, HEAD_DIM, HEAD_DIM), F32),
                        pltpu.VMEM((tb // CHUNK, 1, width), F32)]
                       + [pltpu.VMEM((tb, width), BF16) for _ in range(8)]
                       + [pltpu.VMEM((hb, tb, CHUNK), BF16)],
        compiler_params=_params(("arbitrary", "arbitrary", "arbitrary"), 32 * MIB),
        name="scan_bw" if reverse else "scan_fw",
    )(*args)


def _conv_kernel(cur_ref, prev_ref, next_ref, w_ref, cb_ref, lg_ref, lb_ref, o_ref, ext_ref, acc_ref,
                 *, tb, taps):
    i = pl.program_id(1)
    nt = pl.num_programs(1)
    pad = (taps - 1) // 2
    d_b = cur_ref.shape[1]
    n_lt = d_b // LANES
    zeros = jnp.zeros((CONV_HALO, LANES), F32)
    for lt in range(n_lt):
        lanes = slice(lt * LANES, (lt + 1) * LANES)
        ext_ref[lt, 0:CONV_HALO, :] = jnp.where(i > 0, prev_ref[:, lanes].astype(F32), zeros)
        ext_ref[lt, CONV_HALO:CONV_HALO + tb, :] = cur_ref[:, lanes].astype(F32)
        ext_ref[lt, CONV_HALO + tb:, :] = jnp.where(i < nt - 1, next_ref[:, lanes].astype(F32), zeros)
    rg = 64

    def lane_tile(lt, carry):
        for g in range(tb // rg):
            base = g * rg + CONV_HALO - pad
            acc = jnp.zeros((rg, LANES), F32)
            for j in range(taps):
                acc = acc + w_ref[lt, j:j + 1, :] * ext_ref[lt, base + j:base + j + rg, :]
            acc_ref[lt, g * rg:(g + 1) * rg, :] = acc
        return carry

    lax.fori_loop(0, n_lt, lane_tile, 0)
    tiles = [acc_ref[lt] + cb_ref[:, lt * LANES:(lt + 1) * LANES] for lt in range(n_lt)]
    mu = jnp.sum(sum(tiles), axis=-1, keepdims=True) * (1.0 / d_b)
    cen = [t - mu for t in tiles]
    var = jnp.sum(sum(c * c for c in cen), axis=-1, keepdims=True) * (1.0 / d_b)
    inv = lax.rsqrt(var + EPS)
    for lt in range(n_lt):
        lanes = slice(lt * LANES, (lt + 1) * LANES)
        hn = cen[lt] * inv * lg_ref[:, lanes] + lb_ref[:, lanes]
        o_ref[:, lanes] = _silu(hn).astype(o_ref.dtype)


def _conv_module(hglu, conv_w, conv_b, ln_g, ln_b, nb, seq_len):
    m, d_b = hglu.shape
    taps = conv_w.shape[0]
    assert (taps - 1) // 2 < CONV_HALO
    tb = _tile(seq_len, 256, 64)
    nt = seq_len // tb
    hpb = tb // CONV_HALO
    nh = seq_len // CONV_HALO

    def prev_map(b, i):
        return (b * nh + jnp.maximum(i * hpb - 1, 0), 0)

    def next_map(b, i):
        return (b * nh + jnp.minimum((i + 1) * hpb, nh - 1), 0)

    vec = lambda: pl.BlockSpec((1, d_b), lambda b, i: (0, 0))
    return pl.pallas_call(
        functools.partial(_conv_kernel, tb=tb, taps=taps),
        grid=(nb, nt),
        in_specs=[pl.BlockSpec((tb, d_b), lambda b, i: (b * nt + i, 0)),
                  pl.BlockSpec((CONV_HALO, d_b), prev_map),
                  pl.BlockSpec((CONV_HALO, d_b), next_map),
                  pl.BlockSpec((d_b // LANES, taps, LANES), lambda b, i: (0, 0, 0)),
                  vec(), vec(), vec()],
        out_specs=pl.BlockSpec((tb, d_b), lambda b, i: (b * nt + i, 0)),
        out_shape=jax.ShapeDtypeStruct((m, d_b), BF16),
        scratch_shapes=[pltpu.VMEM((d_b // LANES, tb + 2 * CONV_HALO, LANES), F32),
                        pltpu.VMEM((d_b // LANES, tb, LANES), F32)],
        compiler_params=_params(("arbitrary", "arbitrary"), 12 * tb * d_b * 4 + 8 * MIB),
        name="conv_module",
    )(hglu, hglu, hglu, conv_w.reshape(taps, d_b // LANES, LANES).transpose(1, 0, 2), conv_b.reshape(1, d_b),
      ln_g.reshape(1, d_b), ln_b.reshape(1, d_b))


def _pos_tables(seq_len, dim):
    rows = seq_len // GRID_W
    quarter = dim // 4
    omega = 1.0 / (POS_BASE ** (jnp.arange(quarter, dtype=F32) / quarter))
    ang_r = jnp.arange(rows, dtype=F32)[:, None] * omega
    ang_c = jnp.arange(GRID_W, dtype=F32)[:, None] * omega
    prow = jnp.concatenate([jnp.sin(ang_r), jnp.cos(ang_r)], axis=-1)
    pcol = jnp.concatenate([jnp.sin(ang_c), jnp.cos(ang_c)], axis=-1)
    return prow, pcol


def _mixer_inputs(h, w_in, layer, lbvec, d_a, d_b, d_model):
    z1 = _inproj_main(h, w_in, layer, d_a, d_b, d_model)
    gf, kf = _inproj_forget(h, w_in, layer, lbvec, d_a)
    return z1, gf, kf


def kernel(x, c, ctx, c_ctx, w_mod, b_mod, g_norm1, w_in, lb_logits, g_onorm, w_a, conv_w, conv_b,
           ln_g, ln_b, w_b, w_o, g_norm2, w_ffn_gate, w_ffn_up, w_ffn_down, g_final):
    bsz, seq_len, d_model = x.shape
    ctx_len = ctx.shape[1]
    depth = w_mod.shape[0]
    d_a = lb_logits.shape[-1]
    d_b = conv_b.shape[-1]
    d_ff = w_ffn_gate.shape[-1]
    heads = d_a // HEAD_DIM
    assert bsz + 1 <= 8

    lb_all = jnp.cumsum(jax.nn.softmax(lb_logits.astype(F32), axis=0), axis=0)
    lb_all = lb_all - lb_all[0:1]
    lb_flat = lb_all.reshape(depth, 1, 2 * d_a)
    lbvecs = jnp.stack([jnp.log(lb_flat), jnp.log1p(-lb_flat)], axis=1)

    cond = jnp.zeros((8, d_model), F32).at[:bsz].set(c).at[bsz].set(c_ctx)
    mod = _modvec(cond, w_mod, b_mod)

    ff_pad = (-d_ff) % 1024
    wb = lambda w: w.astype(BF16)
    w_in_b, w_a_b, w_b_b, w_o_b = wb(w_in), wb(w_a), wb(w_b), wb(w_o)
    w_gate_b = _cast_pad(w_ffn_gate, 2, d_ff + ff_pad)
    w_up_b = _cast_pad(w_ffn_up, 2, d_ff + ff_pad)
    w_down_b = _cast_pad(w_ffn_down, 1, d_ff + ff_pad)

    pos = _pos_tables(seq_len, d_model)
    xl = x.reshape(bsz * seq_len, d_model)
    xc = ctx.reshape(bsz * ctx_len, d_model)
    s_zero = jnp.zeros((bsz, heads, HEAD_DIM, HEAD_DIM), F32)
    gate_col0 = 3 * d_a

    for layer in range(depth):
        last = layer == depth - 1
        mods = mod[layer].reshape(8, 6, d_model)
        lat = [mods[:bsz, k][:, None, :] for k in range(6)]
        cm = [jnp.broadcast_to(mods[bsz, k][None, None, :], (bsz, 1, d_model)) for k in range(6)]
        sh1, sc1, gt1, sh2, sc2, gt2 = lat
        sh1c, sc1c, gt1c, sh2c, sc2c, gt2c = cm

        if layer == 0:
            h, xl = _modulate(xl, g_norm1[layer], sh1, sc1, seq_len, pos=pos)
        else:
            h = _modulate(xl, g_norm1[layer], sh1, sc1, seq_len)
        hc = _modulate(xc, g_norm1[layer], sh1c, sc1c, ctx_len)

        z1c, gfc, kfc = _mixer_inputs(hc, w_in_b, layer, lbvecs[layer], d_a, d_b, d_model)
        oc_fw, s_fw = _scan(z1c, gfc, kfc, s_zero, bsz, ctx_len, d_a, reverse=False)
        ac, s_bw = _scan(z1c, gfc, kfc, s_zero, bsz, ctx_len, d_a, reverse=True,
                         fin=(oc_fw, g_onorm[layer]))

        z1, gf, kf = _mixer_inputs(h, w_in_b, layer, lbvecs[layer], d_a, d_b, d_model)
        o_fw, _ = _scan(z1, gf, kf, s_fw, bsz, seq_len, d_a, reverse=False)
        a_in, _ = _scan(z1, gf, kf, s_bw, bsz, seq_len, d_a, reverse=True, fin=(o_fw, g_onorm[layer]))

        hglu = _inproj_glu(h, w_in_b, layer, d_a, d_b)
        b_in = _conv_module(hglu, conv_w[layer], conv_b[layer], ln_g[layer], ln_b[layer], bsz, seq_len)
        mrg = _merge(a_in, b_in, w_a_b, w_b_b, layer, z1, gate_col0)
        xl = _proj_residual(mrg, w_o_b, layer, xl, gt1, seq_len, name="out_proj")
        h2 = _modulate(xl, g_norm2[layer], sh2, sc2, seq_len)
        u = _ffn_up(h2, w_gate_b, w_up_b, layer)
        xl = _proj_residual(u, w_down_b, layer, xl, gt2, seq_len, bm_pref=512, name="ffn_down")

        if not last:
            hgluc = _inproj_glu(hc, w_in_b, layer, d_a, d_b)
            bc = _conv_module(hgluc, conv_w[layer], conv_b[layer], ln_g[layer], ln_b[layer], bsz, ctx_len)
            mrgc = _merge(ac, bc, w_a_b, w_b_b, layer, z1c, gate_col0)
            xc = _proj_residual(mrgc, w_o_b, layer, xc, gt1c, ctx_len, name="out_proj_ctx")
            h2c = _modulate(xc, g_norm2[layer], sh2c, sc2c, ctx_len)
            uc = _ffn_up(h2c, w_gate_b, w_up_b, layer)
            xc = _proj_residual(uc, w_down_b, layer, xc, gt2c, ctx_len, bm_pref=512, name="ffn_down_ctx")

    return _final_norm(xl, g_final).reshape(bsz, seq_len, d_model)
```

```python
import functools

import jax
import jax.numpy as jnp
from jax import lax
from jax.experimental import pallas as pl
from jax.experimental.pallas import tpu as pltpu

EPS = 1e-6
GRID_W = 64
POS_BASE = 10000.0
HEAD_DIM = 128
CHUNK = 64
HALF = CHUNK // 2
SUB = 16
SAFE_STEP_LOG_DECAY = 75.0 / (SUB // 2)
CONV_HALO = 16
LANES = 128
ROW_TILE = 1024
ROW_SUB = 1024
LOG2_E = 1.4426950408889634
MIB = 1 << 20
VMEM_CAP = 60 * MIB

F32 = jnp.float32
BF16 = jnp.bfloat16


def _tile(n, pref, mult):
    t = (min(pref, n) // mult) * mult
    while t >= mult:
        if n % t == 0:
            return t
        t -= mult
    raise ValueError(f"no tile for {n} (pref {pref}, mult {mult})")


def _params(sem, vmem_bytes):
    return pltpu.CompilerParams(dimension_semantics=sem,
                                vmem_limit_bytes=int(min(max(vmem_bytes, 16 * MIB), VMEM_CAP)))


def _sigmoid(x):
    return jax.nn.sigmoid(x)


def _silu(x):
    return x * jax.nn.sigmoid(x)


def _log_sigmoid(x):
    return jnp.minimum(x, 0.0) - jnp.log1p(jnp.exp(-jnp.abs(x)))


def _modvec_kernel(a_ref, w_ref, b_ref, o_ref):
    a = a_ref[...]
    act = (a * _sigmoid(a)).astype(BF16)
    o_ref[...] = jnp.dot(act, w_ref[...].astype(BF16), preferred_element_type=F32) + b_ref[...]


def _modvec(cond_rows, w_mod, b_mod):
    depth, d, md = w_mod.shape
    bn = _tile(md, 512, LANES)
    return pl.pallas_call(
        _modvec_kernel,
        grid=(depth, md // bn),
        in_specs=[pl.BlockSpec((8, d), lambda l, j: (0, 0)),
                  pl.BlockSpec((None, d, bn), lambda l, j: (l, 0, j)),
                  pl.BlockSpec((None, 1, bn), lambda l, j: (l, 0, j))],
        out_specs=pl.BlockSpec((None, 8, bn), lambda l, j: (l, 0, j)),
        out_shape=jax.ShapeDtypeStruct((depth, 8, md), F32),
        compiler_params=_params(("arbitrary", "arbitrary"), 2 * d * bn * 4 + d * bn * 2 + 4 * MIB),
        name="modvec",
    )(cond_rows, w_mod, b_mod.reshape(depth, 1, md))


def _modulate_kernel(*refs, with_pos, bm):
    if with_pos:
        x_ref, g_ref, sh_ref, sc_ref, prow_ref, pcol_ref, h_ref, xo_ref = refs
        half = x_ref.shape[1] // 2
        for r in range(bm // GRID_W):
            rows = slice(r * GRID_W, (r + 1) * GRID_W)
            xo_ref[rows, :half] = x_ref[rows, :half] + prow_ref[r:r + 1, :]
            xo_ref[rows, half:] = x_ref[rows, half:] + pcol_ref[...]
        x = xo_ref[...]
    else:
        x_ref, g_ref, sh_ref, sc_ref, h_ref = refs
        x = x_ref[...]
    ms = jnp.mean(x * x, axis=-1, keepdims=True)
    y = x * lax.rsqrt(ms + EPS) * g_ref[...]
    h_ref[...] = (y * (1.0 + sc_ref[...]) + sh_ref[...]).astype(h_ref.dtype)


def _modulate(x2d, gain, shift, scale, seq_len, pos=None):
    m, d = x2d.shape
    with_pos = pos is not None
    bm = _tile(seq_len, 256, GRID_W if with_pos else 8)
    tpb = seq_len // bm
    in_specs = [pl.BlockSpec((bm, d), lambda i: (i, 0)),
                pl.BlockSpec((1, d), lambda i: (0, 0)),
                pl.BlockSpec((None, 1, d), lambda i: (i // tpb, 0, 0)),
                pl.BlockSpec((None, 1, d), lambda i: (i // tpb, 0, 0))]
    args = [x2d, gain.reshape(1, d), shift, scale]
    out_specs = [pl.BlockSpec((bm, d), lambda i: (i, 0))]
    out_shape = [jax.ShapeDtypeStruct((m, d), BF16)]
    if with_pos:
        prow, pcol = pos
        rpt = bm // GRID_W
        in_specs += [pl.BlockSpec((None, rpt, d // 2), lambda i: (i % tpb, 0, 0)),
                     pl.BlockSpec((GRID_W, d // 2), lambda i: (0, 0))]
        args += [prow.reshape(tpb, rpt, d // 2), pcol]
        out_specs.append(pl.BlockSpec((bm, d), lambda i: (i, 0)))
        out_shape.append(jax.ShapeDtypeStruct((m, d), F32))
    out = pl.pallas_call(
        functools.partial(_modulate_kernel, with_pos=with_pos, bm=bm),
        grid=(m // bm,),
        in_specs=in_specs, out_specs=out_specs, out_shape=out_shape,
        compiler_params=_params(("arbitrary",), 2 * bm * d * (4 + 2 + (4 if with_pos else 0)) + 8 * MIB),
        name="modulate_pos" if with_pos else "modulate",
    )(*args)
    return (out[0], out[1]) if with_pos else out[0]


def _final_norm_kernel(x_ref, g_ref, o_ref):
    x = x_ref[...]
    ms = jnp.mean(x * x, axis=-1, keepdims=True)
    o_ref[...] = x * lax.rsqrt(ms + EPS) * g_ref[...]


def _final_norm(x2d, gain):
    m, d = x2d.shape
    bm = _tile(m, 256, 8)
    return pl.pallas_call(
        _final_norm_kernel,
        grid=(m // bm,),
        in_specs=[pl.BlockSpec((bm, d), lambda i: (i, 0)), pl.BlockSpec((1, d), lambda i: (0, 0))],
        out_specs=pl.BlockSpec((bm, d), lambda i: (i, 0)),
        out_shape=jax.ShapeDtypeStruct((m, d), F32),
        compiler_params=_params(("arbitrary",), 4 * bm * d * 4 + 8 * MIB),
        name="final_norm",
    )(x2d, gain.reshape(1, d))


def _cast_pad_kernel(x_ref, o_ref, *, n_valid):
    keep = pl.program_id(1) < n_valid
    o_ref[...] = jnp.where(keep, x_ref[...], 0.0).astype(o_ref.dtype)


def _cast_pad(w, axis, new_size):
    depth, kdim, ndim = w.shape
    old = w.shape[axis]
    blk = _tile(old, 256, LANES)
    assert new_size % blk == 0
    n_valid, n_total = old // blk, new_size // blk
    if axis == 2:
        block = (None, kdim, blk)
        in_map = lambda l, j: (l, 0, jnp.minimum(j, n_valid - 1))
        out_map = lambda l, j: (l, 0, j)
        out_shape = (depth, kdim, new_size)
    else:
        block = (None, blk, ndim)
        in_map = lambda l, j: (l, jnp.minimum(j, n_valid - 1), 0)
        out_map = lambda l, j: (l, j, 0)
        out_shape = (depth, new_size, ndim)
    return pl.pallas_call(
        functools.partial(_cast_pad_kernel, n_valid=n_valid),
        grid=(depth, n_total),
        in_specs=[pl.BlockSpec(block, in_map)],
        out_specs=pl.BlockSpec(block, out_map),
        out_shape=jax.ShapeDtypeStruct(out_shape, BF16),
        compiler_params=_params(("arbitrary", "arbitrary"), 12 * (kdim if axis == 2 else ndim) * blk + 4 * MIB),
        name="cast_pad",
    )(w)


def _matmul_kernel(*refs, n_a, dots, n_extra, body, bm, sub):
    a_refs = refs[:n_a]
    w_refs = refs[n_a:n_a + len(dots)]
    e_refs = refs[n_a + len(dots):n_a + len(dots) + n_extra]
    o_refs = refs[n_a + len(dots) + n_extra:]
    for r in range(bm // sub):
        rows = slice(r * sub, (r + 1) * sub)
        accs = [jnp.dot(a_refs[ai][rows, :], w_refs[d][...], preferred_element_type=F32)
                for d, ai in enumerate(dots)]
        body(accs, e_refs, o_refs, rows)


def _matmul(a_list, dots, extras, outs, body, *, layer, bm, bn, n_col_blocks, name):
    m = a_list[0].shape[0]
    sub = _tile(bm, ROW_SUB, 8)
    assert m % bm == 0
    assert all(w.shape[1] == a_list[ai].shape[1] for (ai, w, _) in dots)
    in_specs = [pl.BlockSpec((bm, a.shape[1]), lambda i, j: (i, 0)) for a in a_list]
    in_specs += [pl.BlockSpec((None, w.shape[1], bn), (lambda cf: lambda i, j: (layer, 0, cf(j)))(cf))
                 for (_, w, cf) in dots]
    in_specs += [pl.BlockSpec(bs, f) for (_, bs, f) in extras]
    out_specs = [pl.BlockSpec(bs, f) for (_, bs, f) in outs]

    def nbytes(shape, dtype):
        n = 1
        for s in shape:
            n *= (s or 1)
        return n * jnp.dtype(dtype).itemsize

    vmem = 2 * (sum(bm * a.shape[1] * 2 for a in a_list) + sum(w.shape[1] * bn * 2 for (_, w, _) in dots))
    vmem += 2 * sum(nbytes(bs, arr.dtype) for (arr, bs, _) in extras)
    vmem += 2 * sum(nbytes(bs, sd.dtype) for (sd, bs, _) in outs)
    vmem += 3 * len(dots) * sub * bn * 4 + 6 * MIB
    return pl.pallas_call(
        functools.partial(_matmul_kernel, n_a=len(a_list), dots=[ai for (ai, _, _) in dots],
                          n_extra=len(extras), body=body, bm=bm, sub=sub),
        grid=(m // bm, n_col_blocks), in_specs=in_specs, out_specs=out_specs,
        out_shape=[sd for (sd, _, _) in outs],
        compiler_params=_params(("arbitrary", "arbitrary"), vmem),
        name=name,
    )(*a_list, *[w for (_, w, _) in dots], *[arr for (arr, _, _) in extras])


def _inproj_main(h, w_in, layer, d_a, d_b, d_model):
    m = h.shape[0]
    bm = _tile(m, ROW_TILE, 8)
    bn = _tile(d_a, 1024, LANES)
    groups = [(0, d_a, "silu"), (d_a, d_a, "id"), (4 * d_a, d_a, "silu"),
              (5 * d_a + 2 * d_b, 2 * d_model, "sigmoid")]
    ranges, j0 = [], 0
    for (src, width, act) in groups:
        assert src % bn == 0 and width % bn == 0
        ranges.append((j0, j0 + width // bn, src // bn - j0, act))
        j0 += width // bn
    n_blocks = j0

    def col_fn(j):
        blk = j + ranges[0][2]
        for (a0, _, off, _) in ranges[1:]:
            blk = jnp.where(j >= a0, j + off, blk)
        return blk

    def body(accs, e_refs, o_refs, rows):
        j = pl.program_id(1)
        x = accs[0]
        s = _sigmoid(x)
        out = x * s
        for (a0, a1, _, act) in ranges:
            if act != "silu":
                out = jnp.where((j >= a0) & (j < a1), x if act == "id" else s, out)
        o_refs[0][rows, :] = out.astype(BF16)

    n_out = n_blocks * bn
    return _matmul([h], [(0, w_in, col_fn)], [],
                   [(jax.ShapeDtypeStruct((m, n_out), BF16), (bm, bn), lambda i, j: (i, j))],
                   body, layer=layer, bm=bm, bn=bn, n_col_blocks=n_blocks, name="inproj_main")[0]


def _inproj_forget(h, w_in, layer, lbvec, d_a):
    m = h.shape[0]
    bm = _tile(m, ROW_TILE, 8)
    bn = _tile(d_a, 512, LANES)
    off = (2 * d_a) // bn

    def body(accs, e_refs, o_refs, rows):
        x = accs[0]
        log_lb, log_1m_lb = e_refs[0][0], e_refs[0][1]
        a = jnp.broadcast_to(log_lb, x.shape)
        b = log_1m_lb + (jnp.minimum(x, 0.0) - jnp.log(1.0 + jnp.exp(-jnp.abs(x))))
        hi = jnp.maximum(a, b)
        log_f = hi + jnp.log(1.0 + jnp.exp(-jnp.abs(a - b)))
        o_refs[0][rows, :] = log_f
        o_refs[1][rows, :] = (1.0 - jnp.exp(log_f)).astype(BF16)

    return _matmul([h], [(0, w_in, lambda j: j + off)],
                   [(lbvec, (2, 1, bn), lambda i, j: (0, 0, j))],
                   [(jax.ShapeDtypeStruct((m, 2 * d_a), F32), (bm, bn), lambda i, j: (i, j)),
                    (jax.ShapeDtypeStruct((m, 2 * d_a), BF16), (bm, bn), lambda i, j: (i, j))],
                   body, layer=layer, bm=bm, bn=bn, n_col_blocks=(2 * d_a) // bn, name="inproj_forget")


def _inproj_glu(h, w_in, layer, d_a, d_b):
    m = h.shape[0]
    bm = _tile(m, ROW_TILE, 8)
    bn = _tile(d_b, 512, LANES)
    off_a, off_g = (5 * d_a) // bn, (5 * d_a + d_b) // bn

    def body(accs, e_refs, o_refs, rows):
        o_refs[0][rows, :] = (accs[0] * _sigmoid(accs[1])).astype(BF16)

    return _matmul([h], [(0, w_in, lambda j: j + off_a), (0, w_in, lambda j: j + off_g)], [],
                   [(jax.ShapeDtypeStruct((m, d_b), BF16), (bm, bn), lambda i, j: (i, j))],
                   body, layer=layer, bm=bm, bn=bn, n_col_blocks=d_b // bn, name="inproj_glu")[0]


def _merge(a_in, b_in, w_a, w_b, layer, z1, gate_col0):
    m, d_model = a_in.shape[0], w_a.shape[2]
    bm = _tile(m, ROW_TILE, 8)
    bn = _tile(d_model, 512, LANES)
    ga0 = gate_col0 // bn
    gb0 = (gate_col0 + d_model) // bn

    def body(accs, e_refs, o_refs, rows):
        ga = e_refs[0][rows, :].astype(F32)
        gb = e_refs[1][rows, :].astype(F32)
        o_refs[0][rows, :] = (ga * accs[0] + gb * accs[1]).astype(BF16)

    return _matmul([a_in, b_in], [(0, w_a, lambda j: j), (1, w_b, lambda j: j)],
                   [(z1, (bm, bn), lambda i, j: (i, j + ga0)), (z1, (bm, bn), lambda i, j: (i, j + gb0))],
                   [(jax.ShapeDtypeStruct((m, d_model), BF16), (bm, bn), lambda i, j: (i, j))],
                   body, layer=layer, bm=bm, bn=bn, n_col_blocks=d_model // bn, name="merge")[0]


def _proj_residual(a, w, layer, x2d, gate, seq_len, *, bm_pref=ROW_TILE, name):
    m, d_model = x2d.shape
    bm = _tile(seq_len, bm_pref, 8)
    bn = _tile(d_model, 512, LANES)
    tpb = seq_len // bm

    def body(accs, e_refs, o_refs, rows):
        o_refs[0][rows, :] = e_refs[0][rows, :] + e_refs[1][...] * accs[0]

    return _matmul([a], [(0, w, lambda j: j)],
                   [(x2d, (bm, bn), lambda i, j: (i, j)), (gate, (None, 1, bn), lambda i, j: (i // tpb, 0, j))],
                   [(jax.ShapeDtypeStruct((m, d_model), F32), (bm, bn), lambda i, j: (i, j))],
                   body, layer=layer, bm=bm, bn=bn, n_col_blocks=d_model // bn, name=name)[0]


def _ffn_up(h, w_gate, w_up, layer):
    m, d_ff = h.shape[0], w_gate.shape[2]
    bm = _tile(m, ROW_TILE, 8)
    bn = _tile(d_ff, 512, LANES)

    def body(accs, e_refs, o_refs, rows):
        o_refs[0][rows, :] = (_silu(accs[0]) * accs[1]).astype(BF16)

    return _matmul([h], [(0, w_gate, lambda j: j), (0, w_up, lambda j: j)], [],
                   [(jax.ShapeDtypeStruct((m, d_ff), BF16), (bm, bn), lambda i, j: (i, j))],
                   body, layer=layer, bm=bm, bn=bn, n_col_blocks=d_ff // bn, name="ffn_up")[0]


def _cumsum_rows(x, reverse):
    n, grp = x.shape[0], 8
    pos = lax.broadcasted_iota(jnp.int32, x.shape, 0) % grp
    s = 1
    while s < grp:
        if reverse:
            x = x + jnp.where(pos < grp - s, pltpu.roll(x, n - s, 0), 0.0)
        else:
            x = x + jnp.where(pos >= s, pltpu.roll(x, s, 0), 0.0)
        s *= 2
    groups = [x[i * grp:(i + 1) * grp] for i in range(n // grp)]
    total_row = 0 if reverse else grp - 1
    order = range(len(groups) - 2, -1, -1) if reverse else range(1, len(groups))
    off = None
    for i in order:
        prev = groups[i + 1] if reverse else groups[i - 1]
        off = jnp.broadcast_to(prev[total_row:total_row + 1, :], (grp, x.shape[1]))
        groups[i] = groups[i] + off
    return jnp.concatenate(groups, axis=0)


_NT = (((1,), (1,)), ((), ()))
_TN = (((0,), (0,)), ((), ()))


def _scan_refs(b, reverse):
    n_sub = CHUNK // SUB
    bc = lambda r: jnp.broadcast_to(b[r:r + 1, :], (SUB, b.shape[1]))
    zero = jnp.zeros((SUB, b.shape[1]), F32)
    if reverse:
        start = [bc(SUB * (a + 1)) for a in range(n_sub - 1)] + [zero]
        stop = [bc(SUB * a) for a in range(n_sub)]
        mid = [bc(SUB * a + SUB // 2) for a in range(n_sub)]
        rho, b_end = b[HALF:HALF + 1, :], b[0:1, :]
    else:
        start = [zero] + [bc(SUB * a - 1) for a in range(1, n_sub)]
        stop = [bc(SUB * a + SUB - 1) for a in range(n_sub)]
        mid = [bc(SUB * a + SUB // 2 - 1) for a in range(n_sub)]
        rho, b_end = b[HALF - 1:HALF, :], b[CHUNK - 1:CHUNK, :]
    cat = lambda parts: jnp.concatenate(parts, axis=0)
    return cat(start), cat(stop), cat(mid), rho, b_end


def _scan_kernel(*refs, reverse, finalize, tb, hb):
    n_in = 8 if finalize else 5
    q_ref, v_ref, g_ref, k_ref, s0_ref = refs[:5]
    if finalize:
        ofw_ref, sog_ref, gon_ref = refs[5:8]
    o_ref, sfin_ref = refs[n_in:n_in + 2]
    st_ref, dec_ref = refs[n_in + 2:n_in + 4]
    q0_ref, k0_ref, q1_ref, k1_ref, q2_ref, k2_ref, qs_ref, ks_ref, p_ref = refs[n_in + 4:]
    step = pl.program_id(2)
    n_chunks = tb // CHUNK

    @pl.when(step == 0)
    def _():
        st_ref[...] = s0_ref[...]

    g_min = g_ref[...]
    while g_min.shape[0] > 8:
        half = g_min.shape[0] // 2
        g_min = jnp.minimum(g_min[:half], g_min[half:])
    safe = jnp.min(g_min) > -SAFE_STEP_LOG_DECAY

    tri_r = lax.broadcasted_iota(jnp.int32, (CHUNK, CHUNK), 0)
    tri_c = lax.broadcasted_iota(jnp.int32, (CHUNK, CHUNK), 1)
    causal = (tri_c >= tri_r) if reverse else (tri_c <= tri_r)
    pt = (CHUNK - 1 - tri_r) if reverse else tri_r
    ps = (CHUNK - 1 - tri_c) if reverse else tri_c
    sub_t, sub_s = pt // SUB, ps // SUB
    m0 = (sub_t == sub_s) & (ps <= pt)
    m1 = (sub_t == sub_s + 1) & (sub_t % 2 == 1)
    m2 = (pt >= HALF) & (ps < HALF)
    end_row = 0 if reverse else CHUNK - 1

    def emit(rows, lanes, hh, o):
        if finalize:
            tot = o + ofw_ref[rows, lanes]
            ms = jnp.mean(tot * tot, axis=-1, keepdims=True)
            y = tot * lax.rsqrt(ms + EPS) * gon_ref[:, lanes]
            o_ref[rows, lanes] = (y * sog_ref[rows, lanes].astype(F32)).astype(o_ref.dtype)
        else:
            o_ref[rows, lanes] = o

    def tile(c, hh):
        return slice(c * CHUNK, (c + 1) * CHUNK), slice(hh * HEAD_DIM, (hh + 1) * HEAD_DIM)

    def prepare(c, hh):
        rows, lanes = tile(c, hh)
        b = _cumsum_rows(g_ref[rows, lanes] * LOG2_E, reverse)
        q = q_ref[rows, lanes].astype(F32)
        k = k_ref[rows, lanes].astype(F32)
        start, stop, mid, rho, b_end = _scan_refs(b, reverse)
        q0_ref[rows, lanes] = (q * jnp.exp2(b - mid)).astype(BF16)
        k0_ref[rows, lanes] = (k * jnp.exp2(mid - b)).astype(BF16)
        q1_ref[rows, lanes] = (q * jnp.exp2(b - start)).astype(BF16)
        k1_ref[rows, lanes] = (k * jnp.exp2(stop - b)).astype(BF16)
        e2 = jnp.exp2(-jnp.abs(b - rho))
        q2_ref[rows, lanes] = (q * e2).astype(BF16)
        k2_ref[rows, lanes] = (k * e2).astype(BF16)
        qs_ref[rows, lanes] = (q * jnp.exp2(b)).astype(BF16)
        ks_ref[rows, lanes] = (k * jnp.exp2(b_end - b)).astype(BF16)
        dec_ref[c, :, lanes] = jnp.exp2(b_end)

    nt = lambda a, bb: lax.dot_general(a, bb, _NT, preferred_element_type=F32)

    def scores(c, hh):
        rows, lanes = tile(c, hh)
        s0 = nt(q0_ref[rows, lanes], k0_ref[rows, lanes])
        s1 = nt(q1_ref[rows, lanes], k1_ref[rows, lanes])
        s2 = nt(q2_ref[rows, lanes], k2_ref[rows, lanes])
        p_ref[hh, rows, :] = jnp.where(m0, s0, jnp.where(m1, s1, jnp.where(m2, s2, 0.0))).astype(BF16)

    def instance(c, hh):
        rows, lanes = tile(c, hh)
        v = v_ref[rows, lanes]
        st = st_ref[hh]
        o = nt(qs_ref[rows, lanes], st.astype(BF16)) + jnp.dot(p_ref[hh, rows, :], v,
                                                                preferred_element_type=F32)
        upd = lax.dot_general(v, ks_ref[rows, lanes], _TN, preferred_element_type=F32)
        st_ref[hh] = st * dec_ref[c, :, lanes] + upd
        emit(rows, lanes, hh, o)

    def exact_instance(c, hh):
        rows, lanes = tile(c, hh)
        b = _cumsum_rows(g_ref[rows, lanes], reverse)
        q = q_ref[rows, lanes].astype(F32)
        k = k_ref[rows, lanes].astype(F32)
        v = v_ref[rows, lanes]
        st = st_ref[hh]
        b_end = b[end_row:end_row + 1, :]
        row_id = lax.broadcasted_iota(jnp.int32, b.shape, 0)

        def col(s, sc):
            pick = row_id == s
            b_s = jnp.sum(jnp.where(pick, b, 0.0), axis=0, keepdims=True)
            k_s = jnp.sum(jnp.where(pick, k, 0.0), axis=0, keepdims=True)
            d = jnp.exp(jnp.minimum(b - b_s, 0.0))
            cs = jnp.sum(q * d * k_s, axis=-1, keepdims=True)
            return jnp.where(tri_c == s, cs, sc)

        scores = lax.fori_loop(0, CHUNK, col, jnp.zeros((CHUNK, CHUNK), F32))
        o_int = lax.dot_general((q * jnp.exp(b)).astype(BF16), st.astype(BF16), _NT,
                                preferred_element_type=F32)
        upd = lax.dot_general(v, (k * jnp.exp(b_end - b)).astype(BF16), _TN,
                              preferred_element_type=F32)
        p = jnp.where(causal, scores, 0.0).astype(BF16)
        o = o_int + jnp.dot(p, v, preferred_element_type=F32)
        st_ref[hh] = st * jnp.exp(b_end) + upd
        emit(rows, lanes, hh, o)

    def run(fast):
        order = range(n_chunks - 1, -1, -1) if reverse else range(n_chunks)
        if fast:
            for stage in (prepare, scores):
                for c in order:
                    for hh in range(hb):
                        stage(c, hh)
        for c in order:
            for hh in range(hb):
                (instance if fast else exact_instance)(c, hh)

    @pl.when(safe)
    def _():
        run(True)

    @pl.when(jnp.logical_not(safe))
    def _():
        run(False)

    @pl.when(step == pl.num_programs(2) - 1)
    def _():
        sfin_ref[...] = st_ref[...]


def _scan(z1, gf, kf, s0, nb, seq_len, d_a, *, reverse, fin=None):
    heads = d_a // HEAD_DIM
    hb = 4 if heads % 4 == 0 else 1
    width = hb * HEAD_DIM
    tb = _tile(seq_len, 512, CHUNK)
    nt = seq_len // tb
    ng = heads // hb
    gcols = d_a // width
    dirn = 1 if reverse else 0

    def row(b, i):
        return b * nt + ((nt - 1 - i) if reverse else i)

    in_specs = [pl.BlockSpec((tb, width), lambda b, h, i: (row(b, i), h)),
                pl.BlockSpec((tb, width), lambda b, h, i: (row(b, i), gcols + h)),
                pl.BlockSpec((tb, width), lambda b, h, i: (row(b, i), dirn * gcols + h)),
                pl.BlockSpec((tb, width), lambda b, h, i: (row(b, i), dirn * gcols + h)),
                pl.BlockSpec((None, hb, HEAD_DIM, HEAD_DIM), lambda b, h, i: (b, h, 0, 0))]
    args = [z1, z1, gf, kf, s0]
    if fin is not None:
        o_fw, g_onorm = fin
        in_specs += [pl.BlockSpec((tb, width), lambda b, h, i: (row(b, i), h)),
                     pl.BlockSpec((tb, width), lambda b, h, i: (row(b, i), 2 * gcols + h)),
                     pl.BlockSpec((1, width), lambda b, h, i: (0, h))]
        args += [o_fw, z1, g_onorm.reshape(1, d_a)]
    out_dtype = BF16 if fin is not None else F32
    out_specs = [pl.BlockSpec((tb, width), lambda b, h, i: (row(b, i), h)),
                 pl.BlockSpec((None, hb, HEAD_DIM, HEAD_DIM), lambda b, h, i: (b, h, 0, 0))]
    out_shape = [jax.ShapeDtypeStruct((nb * seq_len, d_a), out_dtype),
                 jax.ShapeDtypeStruct((nb, heads, HEAD_DIM, HEAD_DIM), F32)]
    return pl.pallas_call(
        functools.partial(_scan_kernel, reverse=reverse, finalize=fin is not None, tb=tb, hb=hb),
        grid=(nb, ng, nt),
        in_specs=in_specs, out_specs=out_specs, out_shape=out_shape,
        scratch_shapes=[pltpu.VMEM((hb, HEAD_DIM, HEAD_DIM), F32),
                        pltpu.VMEM((tb // CHUNK, 1, width), F32)]
                       + [pltpu.VMEM((tb, width), BF16) for _ in range(8)]
                       + [pltpu.VMEM((hb, tb, CHUNK), BF16)],
        compiler_params=_params(("arbitrary", "arbitrary", "arbitrary"), 32 * MIB),
        name="scan_bw" if reverse else "scan_fw",
    )(*args)


def _conv_kernel(cur_ref, prev_ref, next_ref, w_ref, cb_ref, lg_ref, lb_ref, o_ref, ext_ref, acc_ref,
                 *, tb, taps):
    i = pl.program_id(1)
    nt = pl.num_programs(1)
    pad = (taps - 1) // 2
    d_b = cur_ref.shape[1]
    n_lt = d_b // LANES
    zeros = jnp.zeros((CONV_HALO, LANES), F32)
    for lt in range(n_lt):
        lanes = slice(lt * LANES, (lt + 1) * LANES)
        ext_ref[lt, 0:CONV_HALO, :] = jnp.where(i > 0, prev_ref[:, lanes].astype(F32), zeros)
        ext_ref[lt, CONV_HALO:CONV_HALO + tb, :] = cur_ref[:, lanes].astype(F32)
        ext_ref[lt, CONV_HALO + tb:, :] = jnp.where(i < nt - 1, next_ref[:, lanes].astype(F32), zeros)
    rg = 64

    def lane_tile(lt, carry):
        for g in range(tb // rg):
            base = g * rg + CONV_HALO - pad
            acc = jnp.zeros((rg, LANES), F32)
            for j in range(taps):
                acc = acc + w_ref[lt, j:j + 1, :] * ext_ref[lt, base + j:base + j + rg, :]
            acc_ref[lt, g * rg:(g + 1) * rg, :] = acc
        return carry

    lax.fori_loop(0, n_lt, lane_tile, 0)
    tiles = [acc_ref[lt] + cb_ref[:, lt * LANES:(lt + 1) * LANES] for lt in range(n_lt)]
    mu = jnp.sum(sum(tiles), axis=-1, keepdims=True) * (1.0 / d_b)
    cen = [t - mu for t in tiles]
    var = jnp.sum(sum(c * c for c in cen), axis=-1, keepdims=True) * (1.0 / d_b)
    inv = lax.rsqrt(var + EPS)
    for lt in range(n_lt):
        lanes = slice(lt * LANES, (lt + 1) * LANES)
        hn = cen[lt] * inv * lg_ref[:, lanes] + lb_ref[:, lanes]
        o_ref[:, lanes] = _silu(hn).astype(o_ref.dtype)


def _conv_module(hglu, conv_w, conv_b, ln_g, ln_b, nb, seq_len):
    m, d_b = hglu.shape
    taps = conv_w.shape[0]
    assert (taps - 1) // 2 < CONV_HALO
    tb = _tile(seq_len, 256, 64)
    nt = seq_len // tb
    hpb = tb // CONV_HALO
    nh = seq_len // CONV_HALO

    def prev_map(b, i):
        return (b * nh + jnp.maximum(i * hpb - 1, 0), 0)

    def next_map(b, i):
        return (b * nh + jnp.minimum((i + 1) * hpb, nh - 1), 0)

    vec = lambda: pl.BlockSpec((1, d_b), lambda b, i: (0, 0))
    return pl.pallas_call(
        functools.partial(_conv_kernel, tb=tb, taps=taps),
        grid=(nb, nt),
        in_specs=[pl.BlockSpec((tb, d_b), lambda b, i: (b * nt + i, 0)),
                  pl.BlockSpec((CONV_HALO, d_b), prev_map),
                  pl.BlockSpec((CONV_HALO, d_b), next_map),
                  pl.BlockSpec((d_b // LANES, taps, LANES), lambda b, i: (0, 0, 0)),
                  vec(), vec(), vec()],
        out_specs=pl.BlockSpec((tb, d_b), lambda b, i: (b * nt + i, 0)),
        out_shape=jax.ShapeDtypeStruct((m, d_b), BF16),
        scratch_shapes=[pltpu.VMEM((d_b // LANES, tb + 2 * CONV_HALO, LANES), F32),
                        pltpu.VMEM((d_b // LANES, tb, LANES), F32)],
        compiler_params=_params(("arbitrary", "arbitrary"), 12 * tb * d_b * 4 + 8 * MIB),
        name="conv_module",
    )(hglu, hglu, hglu, conv_w.reshape(taps, d_b // LANES, LANES).transpose(1, 0, 2), conv_b.reshape(1, d_b),
      ln_g.reshape(1, d_b), ln_b.reshape(1, d_b))


def _pos_tables(seq_len, dim):
    rows = seq_len // GRID_W
    quarter = dim // 4
    omega = 1.0 / (POS_BASE ** (jnp.arange(quarter, dtype=F32) / quarter))
    ang_r = jnp.arange(rows, dtype=F32)[:, None] * omega
    ang_c = jnp.arange(GRID_W, dtype=F32)[:, None] * omega
    prow = jnp.concatenate([jnp.sin(ang_r), jnp.cos(ang_r)], axis=-1)
    pcol = jnp.concatenate([jnp.sin(ang_c), jnp.cos(ang_c)], axis=-1)
    return prow, pcol


def _mixer_inputs(h, w_in, layer, lbvec, d_a, d_b, d_model):
    z1 = _inproj_main(h, w_in, layer, d_a, d_b, d_model)
    gf, kf = _inproj_forget(h, w_in, layer, lbvec, d_a)
    return z1, gf, kf


def kernel(x, c, ctx, c_ctx, w_mod, b_mod, g_norm1, w_in, lb_logits, g_onorm, w_a, conv_w, conv_b,
           ln_g, ln_b, w_b, w_o, g_norm2, w_ffn_gate, w_ffn_up, w_ffn_down, g_final):
    bsz, seq_len, d_model = x.shape
    ctx_len = ctx.shape[1]
    depth = w_mod.shape[0]
    d_a = lb_logits.shape[-1]
    d_b = conv_b.shape[-1]
    d_ff = w_ffn_gate.shape[-1]
    heads = d_a // HEAD_DIM
    assert bsz + 1 <= 8

    lb_all = jnp.cumsum(jax.nn.softmax(lb_logits.astype(F32), axis=0), axis=0)
    lb_all = lb_all - lb_all[0:1]
    lb_flat = lb_all.reshape(depth, 1, 2 * d_a)
    lbvecs = jnp.stack([jnp.log(lb_flat), jnp.log1p(-lb_flat)], axis=1)

    cond = jnp.zeros((8, d_model), F32).at[:bsz].set(c).at[bsz].set(c_ctx)
    mod = _modvec(cond, w_mod, b_mod)

    ff_pad = (-d_ff) % 1024
    wb = lambda w: w.astype(BF16)
    w_in_b, w_a_b, w_b_b, w_o_b = wb(w_in), wb(w_a), wb(w_b), wb(w_o)
    w_gate_b = _cast_pad(w_ffn_gate, 2, d_ff + ff_pad)
    w_up_b = _cast_pad(w_ffn_up, 2, d_ff + ff_pad)
    w_down_b = _cast_pad(w_ffn_down, 1, d_ff + ff_pad)

    pos = _pos_tables(seq_len, d_model)
    xl = x.reshape(bsz * seq_len, d_model)
    xc = ctx.reshape(bsz * ctx_len, d_model)
    s_zero = jnp.zeros((bsz, heads, HEAD_DIM, HEAD_DIM), F32)
    gate_col0 = 3 * d_a

    for layer in range(depth):
        last = layer == depth - 1
        mods = mod[layer].reshape(8, 6, d_model)
        lat = [mods[:bsz, k][:, None, :] for k in range(6)]
        cm = [jnp.broadcast_to(mods[bsz, k][None, None, :], (bsz, 1, d_model)) for k in range(6)]
        sh1, sc1, gt1, sh2, sc2, gt2 = lat
        sh1c, sc1c, gt1c, sh2c, sc2c, gt2c = cm

        if layer == 0:
            h, xl = _modulate(xl, g_norm1[layer], sh1, sc1, seq_len, pos=pos)
        else:
            h = _modulate(xl, g_norm1[layer], sh1, sc1, seq_len)
        hc = _modulate(xc, g_norm1[layer], sh1c, sc1c, ctx_len)

        z1c, gfc, kfc = _mixer_inputs(hc, w_in_b, layer, lbvecs[layer], d_a, d_b, d_model)
        oc_fw, s_fw = _scan(z1c, gfc, kfc, s_zero, bsz, ctx_len, d_a, reverse=False)
        ac, s_bw = _scan(z1c, gfc, kfc, s_zero, bsz, ctx_len, d_a, reverse=True,
                         fin=(oc_fw, g_onorm[layer]))

        z1, gf, kf = _mixer_inputs(h, w_in_b, layer, lbvecs[layer], d_a, d_b, d_model)
        o_fw, _ = _scan(z1, gf, kf, s_fw, bsz, seq_len, d_a, reverse=False)
        a_in, _ = _scan(z1, gf, kf, s_bw, bsz, seq_len, d_a, reverse=True, fin=(o_fw, g_onorm[layer]))

        hglu = _inproj_glu(h, w_in_b, layer, d_a, d_b)
        b_in = _conv_module(hglu, conv_w[layer], conv_b[layer], ln_g[layer], ln_b[layer], bsz, seq_len)
        mrg = _merge(a_in, b_in, w_a_b, w_b_b, layer, z1, gate_col0)
        xl = _proj_residual(mrg, w_o_b, layer, xl, gt1, seq_len, name="out_proj")
        h2 = _modulate(xl, g_norm2[layer], sh2, sc2, seq_len)
        u = _ffn_up(h2, w_gate_b, w_up_b, layer)
        xl = _proj_residual(u, w_down_b, layer, xl, gt2, seq_len, bm_pref=512, name="ffn_down")

        if not last:
            hgluc = _inproj_glu(hc, w_in_b, layer, d_a, d_b)
            bc = _conv_module(hgluc, conv_w[layer], conv_b[layer], ln_g[layer], ln_b[layer], bsz, ctx_len)
            mrgc = _merge(ac, bc, w_a_b, w_b_b, layer, z1c, gate_col0)
            xc = _proj_residual(mrgc, w_o_b, layer, xc, gt1c, ctx_len, name="out_proj_ctx")
            h2c = _modulate(xc, g_norm2[layer], sh2c, sc2c, ctx_len)
            uc = _ffn_up(h2c, w_gate_b, w_up_b, layer)
            xc = _proj_residual(uc, w_down_b, layer, xc, gt2c, ctx_len, bm_pref=512, name="ffn_down_ctx")

    return _final_norm(xl, g_final).reshape(bsz, seq_len, d_model)
```
